```python
import math
import jax, jax.numpy as jnp
from jax import lax
import numpy as np

D_MODEL = 4096
BATCH = 4
SEQ = 4096
DEPTH = 1

HEAD_DIM = 128
HEADS_PER_GROUP = 8
PATTERNS = ((128, 1), (512, 4), (2048, 16))
N_GROUPS = len(PATTERNS)
ATT_W = N_GROUPS * HEADS_PER_GROUP * HEAD_DIM
BLK = 128
LRU_W = D_MODEL
LRU_BLOCKS = 16
LRU_BW = LRU_W // LRU_BLOCKS
LRU_C = 8.0
LRU_CONV = 4
FFN_DIM = 3 * D_MODEL
FFN_CONV = 3
PLE_DIM = 256
IN_SIZES = (ATT_W, ATT_W, ATT_W, LRU_W, LRU_W, D_MODEL, D_MODEL)
IN_COLS = sum(IN_SIZES)
EPS = 1e-6

kernel_name = "hybrid_dilated_attn_rglru_convffn_ple"


def rmsnorm(x, g):
    xf = x.astype(jnp.float32)
    y = xf * lax.rsqrt(jnp.mean(xf * xf, axis=-1, keepdims=True) + EPS)
    return (y * g.astype(jnp.float32)).astype(x.dtype)


def causal_dwconv(x, w, b):
    K = w.shape[0]
    S = x.shape[1]
    xp = jnp.pad(x, ((0, 0), (K - 1, 0), (0, 0)))
    y = xp[:, 0:S] * w[0]
    for k in range(1, K):
        y = y + xp[:, k:k + S] * w[k]
    return y + b


def dilated_causal_attention(q, k, v, dilation, n_back):
    B, S, H, Dh = q.shape
    L = S // dilation
    nb = -(-L // BLK)
    pad_l = nb * BLK - L

    def sub(a):
        return a.reshape(B, L, dilation, H, Dh).transpose(0, 2, 3, 1, 4)

    qs = jnp.pad(sub(q), ((0, 0), (0, 0), (0, 0), (0, pad_l), (0, 0))).reshape(B, dilation, H, nb, BLK, Dh)

    def band(a):
        a = jnp.pad(sub(a), ((0, 0), (0, 0), (0, 0), (BLK, pad_l), (0, 0))).reshape(B, dilation, H, nb + 1, BLK, Dh)
        return jnp.concatenate([a[:, :, :, :-1], a[:, :, :, 1:]], axis=-2)

    kb, vb = band(k), band(v)
    s = jnp.einsum('brhcid,brhcjd->brhcij', qs, kb).astype(jnp.float32) * (Dh ** -0.5)
    qi = jnp.arange(BLK)[:, None]
    km = jnp.arange(2 * BLK)[None, :]
    cb = jnp.arange(nb)[:, None, None]
    back = BLK + qi - km
    valid = (back >= 0) & (back <= n_back) & ((cb - 1) * BLK + km >= 0)
    s = jnp.where(valid, s, jnp.finfo(jnp.float32).min)
    mx = jnp.max(s, axis=-1, keepdims=True)
    pr = jnp.exp(s - mx)
    den = jnp.sum(pr, axis=-1)
    o = jnp.einsum('brhcij,brhcjd->brhcid', pr, vb.astype(jnp.float32)) / den[..., None]
    lse = mx[..., 0] + jnp.log(den)
    o = o.reshape(B, dilation, H, nb * BLK, Dh)[:, :, :, :L].transpose(0, 3, 1, 2, 4).reshape(B, S, H, Dh)
    lse = lse.reshape(B, dilation, H, nb * BLK)[:, :, :, :L].transpose(0, 3, 1, 2).reshape(B, S, H)
    return o, lse


def rglru(xc, r, i, lam):
    log_a = -LRU_C * r.astype(jnp.float32) * jax.nn.softplus(-lam.astype(jnp.float32))
    a = jnp.exp(log_a)
    mult = jnp.sqrt(-jnp.expm1(2.0 * log_a))
    b = mult * i.astype(jnp.float32) * xc.astype(jnp.float32)

    def step(h, ab):
        a_t, b_t = ab
        h = a_t * h + b_t
        return h, h

    h0 = jnp.zeros((xc.shape[0], xc.shape[2]), jnp.float32)
    _, hs = lax.scan(step, h0, (a.transpose(1, 0, 2), b.transpose(1, 0, 2)))
    return hs.transpose(1, 0, 2)


def setup_inputs(seed: int = 0) -> dict:
    key = jax.random.key(seed)
    ks = jax.random.split(key, 26)
    f32 = jnp.float32

    def nrm(k, shape, fan_in):
        return jax.random.normal(k, shape, f32) * (fan_in ** -0.5)

    def gain(k, shape):
        return 1.0 + 0.05 * jax.random.normal(k, shape, f32)

    def bias(k, shape):
        return 0.01 * jax.random.normal(k, shape, f32)

    u = jax.random.uniform(ks[11], (DEPTH, LRU_W), f32, minval=0.9, maxval=0.999)
    a0 = u ** (1.0 / LRU_C)
    lru_lambda = jnp.log(a0) - jnp.log1p(-a0)
    return {
        "x": jax.random.normal(ks[0], (BATCH, SEQ, D_MODEL), f32),
        "p": jax.random.normal(ks[1], (DEPTH, BATCH, SEQ, PLE_DIM), f32),
        "mix_norm_g": gain(ks[2], (DEPTH, D_MODEL)),
        "w_in": nrm(ks[3], (DEPTH, D_MODEL, IN_COLS), D_MODEL),
        "lru_conv_w": nrm(ks[4], (DEPTH, LRU_CONV, LRU_W), LRU_CONV),
        "lru_conv_b": bias(ks[5], (DEPTH, LRU_W)),
        "w_rgate": nrm(ks[6], (DEPTH, LRU_BLOCKS, LRU_BW, LRU_BW), LRU_BW),
        "b_rgate": bias(ks[7], (DEPTH, LRU_W)),
        "w_igate": nrm(ks[8], (DEPTH, LRU_BLOCKS, LRU_BW, LRU_BW), LRU_BW),
        "b_igate": bias(ks[9], (DEPTH, LRU_W)),
        "lru_lambda": lru_lambda,
        "w_att_branch": nrm(ks[12], (DEPTH, ATT_W, D_MODEL), ATT_W),
        "w_lru_branch": nrm(ks[13], (DEPTH, LRU_W, D_MODEL), LRU_W),
        "w_out": nrm(ks[14], (DEPTH, D_MODEL, D_MODEL), D_MODEL),
        "ffn_norm_g": gain(ks[15], (DEPTH, D_MODEL)),
        "w_up": nrm(ks[16], (DEPTH, D_MODEL, 2 * FFN_DIM), D_MODEL),
        "ffn_conv_w": nrm(ks[17], (DEPTH, FFN_CONV, 2 * FFN_DIM), FFN_CONV),
        "ffn_conv_b": bias(ks[18], (DEPTH, 2 * FFN_DIM)),
        "w_down": nrm(ks[19], (DEPTH, FFN_DIM, D_MODEL), FFN_DIM),
        "ple_proj": nrm(ks[20], (DEPTH, PLE_DIM, D_MODEL), PLE_DIM),
        "ple_norm_g": gain(ks[21], (DEPTH, D_MODEL)),
        "ple_gate_norm_g": gain(ks[22], (DEPTH, D_MODEL)),
        "ple_gate_w": nrm(ks[23], (DEPTH, D_MODEL, D_MODEL), D_MODEL),
        "final_norm_g": gain(ks[24], (D_MODEL,)),
    }


def reference(x, p, mix_norm_g, w_in, lru_conv_w, lru_conv_b, w_rgate, b_rgate, w_igate, b_igate,
              lru_lambda, w_att_branch, w_lru_branch, w_out, ffn_norm_g, w_up, ffn_conv_w, ffn_conv_b,
              w_down, ple_proj, ple_norm_g, ple_gate_norm_g, ple_gate_w, final_norm_g):
    B, S, _ = x.shape
    offs = [int(o) for o in np.cumsum(IN_SIZES)[:-1]]
    for l in range(DEPTH):
        h = rmsnorm(x, mix_norm_g[l])
        proj = h @ w_in[l]
        q, k, v, xr, yr, ga, gr = jnp.split(proj, offs, axis=-1)
        q = q.reshape(B, S, N_GROUPS, HEADS_PER_GROUP, HEAD_DIM)
        k = k.reshape(B, S, N_GROUPS, HEADS_PER_GROUP, HEAD_DIM)
        v = v.reshape(B, S, N_GROUPS, HEADS_PER_GROUP, HEAD_DIM)
        outs, lses = [], []
        for g, (win, dil) in enumerate(PATTERNS):
            o_g, lse_g = dilated_causal_attention(q[:, :, g], k[:, :, g], v[:, :, g], dil, win // dil)
            outs.append(o_g)
            lses.append(lse_g)
        o = jnp.stack(outs, axis=2)
        wgt = jax.nn.softmax(jnp.stack(lses, axis=2), axis=2)
        att = (o * wgt[..., None]).astype(x.dtype).reshape(B, S, ATT_W)

        xc = causal_dwconv(xr, lru_conv_w[l], lru_conv_b[l])
        xcb = xc.reshape(B, S, LRU_BLOCKS, LRU_BW)
        r = jax.nn.sigmoid(jnp.einsum('bsni,nij->bsnj', xcb, w_rgate[l]).reshape(B, S, LRU_W) + b_rgate[l])
        i = jax.nn.sigmoid(jnp.einsum('bsni,nij->bsnj', xcb, w_igate[l]).reshape(B, S, LRU_W) + b_igate[l])
        lru = rglru(xc, r, i, lru_lambda[l]).astype(x.dtype) * jax.nn.gelu(yr)

        merged = jax.nn.sigmoid(ga) * (att @ w_att_branch[l]) + jax.nn.sigmoid(gr) * (lru @ w_lru_branch[l])
        x = x + merged @ w_out[l]

        h = rmsnorm(x, ffn_norm_g[l])
        up = causal_dwconv(h @ w_up[l], ffn_conv_w[l], ffn_conv_b[l])
        gate, val = jnp.split(up, 2, axis=-1)
        x = x + (jax.nn.gelu(gate) * val) @ w_down[l]

        e = rmsnorm(p[l].astype(x.dtype) @ ple_proj[l], ple_norm_g[l])
        x = x + jax.nn.sigmoid(rmsnorm(x, ple_gate_norm_g[l]) @ ple_gate_w[l]) * e
    return rmsnorm(x, final_norm_g)
```

```python
import functools
import math

import jax
import jax.numpy as jnp
from jax import lax
from jax.experimental import pallas as pl
from jax.experimental.pallas import tpu as pltpu

F32 = jnp.float32
BF16 = jnp.bfloat16

D_MODEL = 4096
HEAD_DIM = 128
HEADS_PER_GROUP = 8
PATTERNS = ((128, 1), (512, 4), (2048, 16))
N_GROUPS = len(PATTERNS)
GROUP_W = HEADS_PER_GROUP * HEAD_DIM
ATT_W = N_GROUPS * GROUP_W
ATT_BLK = 128
LRU_W = D_MODEL
LRU_BLOCKS = 16
LRU_BW = LRU_W // LRU_BLOCKS
LRU_C = 8.0
LRU_CONV = 4
FFN_DIM = 3 * D_MODEL
FFN_CONV = 3
PLE_DIM = 256
EPS = 1e-6
OFF_Q, OFF_K, OFF_V = 0, ATT_W, 2 * ATT_W
OFF_XR = 3 * ATT_W
OFF_YR = OFF_XR + LRU_W
OFF_GA = OFF_YR + LRU_W
OFF_GR = OFF_GA + D_MODEL
IN_COLS = OFF_GR + D_MODEL

V7X_VMEM_BYTES = 64 * 1024 * 1024
V7X_SUBLANES = 8
V7X_LANES = 128
MIB = 1024 * 1024


def _params(n_axes, vmem_mib):
    assert vmem_mib * MIB < V7X_VMEM_BYTES
    return pltpu.CompilerParams(
        dimension_semantics=("arbitrary",) * n_axes,
        vmem_limit_bytes=vmem_mib * MIB,
    )


def _sigmoid(v):
    return 1.0 / (1.0 + jnp.exp(-v))


def _gelu_tanh(v):
    c = math.sqrt(2.0 / math.pi)
    return v * (0.5 * (1.0 + jnp.tanh(c * (v + 0.044715 * (v * v * v)))))


def _rmsnorm(v, g):
    return v * lax.rsqrt(jnp.mean(v * v, axis=-1, keepdims=True) + EPS) * g


def _dot(a, b):
    return jnp.dot(a, b, preferred_element_type=F32)


IN_TM, IN_TN = 512, 1024


def _in_proj_kernel(x_ref, g_ref, w_ref, o_ref, h_ref):
    j = pl.program_id(1)

    @pl.when(j == 0)
    def _():
        h_ref[...] = _rmsnorm(x_ref[...], g_ref[...]).astype(BF16)

    acc = _dot(h_ref[...], w_ref[...])
    j_yr, j_ga = OFF_YR // IN_TN, OFF_GA // IN_TN

    @pl.when(j < j_yr)
    def _():
        o_ref[...] = acc.astype(BF16)

    @pl.when((j >= j_yr) & (j < j_ga))
    def _():
        o_ref[...] = _gelu_tanh(acc).astype(BF16)

    @pl.when(j >= j_ga)
    def _():
        o_ref[...] = _sigmoid(acc).astype(BF16)


def _in_proj(x2, g, w_bf):
    t = x2.shape[0]
    assert OFF_YR % IN_TN == 0 and OFF_GA % IN_TN == 0
    return pl.pallas_call(
        _in_proj_kernel,
        out_shape=jax.ShapeDtypeStruct((t, IN_COLS), BF16),
        grid=(t // IN_TM, IN_COLS // IN_TN),
        in_specs=[
            pl.BlockSpec((IN_TM, D_MODEL), lambda i, j: (i, 0)),
            pl.BlockSpec((1, D_MODEL), lambda i, j: (0, 0)),
            pl.BlockSpec((D_MODEL, IN_TN), lambda i, j: (0, j)),
        ],
        out_specs=pl.BlockSpec((IN_TM, IN_TN), lambda i, j: (i, j)),
        scratch_shapes=[pltpu.VMEM((IN_TM, D_MODEL), BF16)],
        compiler_params=_params(2, 56),
        name="in_proj",
    )(x2, g, w_bf)


def _attn_kernel(q_ref, kp_ref, kc_ref, vp_ref, vc_ref, o_ref, lse_ref):
    c = pl.program_id(2)
    qi = lax.broadcasted_iota(jnp.int32, (ATT_BLK, ATT_BLK), 0)
    kj = lax.broadcasted_iota(jnp.int32, (ATT_BLK, ATT_BLK), 1)
    mask_c = kj <= qi
    mask_p = (kj >= qi) & (c > 0)
    neg = jnp.finfo(F32).min
    scale = HEAD_DIM ** -0.5
    nt = (((1,), (1,)), ((), ()))
    lse_tile = jnp.zeros((ATT_BLK, V7X_LANES), F32)
    for h in range(HEADS_PER_GROUP):
        sl = slice(h * HEAD_DIM, (h + 1) * HEAD_DIM)
        q = q_ref[:, sl]
        s_c = lax.dot_general(q, kc_ref[:, sl], nt, preferred_element_type=F32) * scale
        s_p = lax.dot_general(q, kp_ref[:, sl], nt, preferred_element_type=F32) * scale
        s_c = jnp.where(mask_c, s_c, neg)
        s_p = jnp.where(mask_p, s_p, neg)
        mx = jnp.maximum(jnp.max(s_c, axis=-1, keepdims=True), jnp.max(s_p, axis=-1, keepdims=True))
        p_c = jnp.exp(s_c - mx)
        p_p = jnp.exp(s_p - mx)
        den = jnp.sum(p_c, axis=-1, keepdims=True) + jnp.sum(p_p, axis=-1, keepdims=True)
        o = (_dot(p_c.astype(BF16), vc_ref[:, sl]) + _dot(p_p.astype(BF16), vp_ref[:, sl])) / den
        o_ref[:, sl] = o.astype(BF16)
        lse = mx + jnp.log(den)
        lse_tile = jnp.where(kj == h, lse, lse_tile)
    lse_ref[...] = lse_tile


def _attention_group(proj3, g, dil):
    b, s, _ = proj3.shape
    l = s // dil
    assert l % ATT_BLK == 0
    nb = l // ATT_BLK
    cb = IN_COLS // GROUP_W
    pv = proj3.reshape(b, l, dil * IN_COLS)

    def col(off):
        return (off + g * GROUP_W) // GROUP_W

    def cur(off):
        return pl.BlockSpec((None, ATT_BLK, GROUP_W), lambda bi, r, c: (bi, c, r * cb + col(off)))

    def prev(off):
        return pl.BlockSpec((None, ATT_BLK, GROUP_W),
                            lambda bi, r, c: (bi, jnp.maximum(c - 1, 0), r * cb + col(off)))

    o, lse = pl.pallas_call(
        _attn_kernel,
        out_shape=(jax.ShapeDtypeStruct((b, l, dil * GROUP_W), BF16),
                   jax.ShapeDtypeStruct((b, l, dil * V7X_LANES), F32)),
        grid=(b, dil, nb),
        in_specs=[cur(OFF_Q), prev(OFF_K), cur(OFF_K), prev(OFF_V), cur(OFF_V)],
        out_specs=(pl.BlockSpec((None, ATT_BLK, GROUP_W), lambda bi, r, c: (bi, c, r)),
                   pl.BlockSpec((None, ATT_BLK, V7X_LANES), lambda bi, r, c: (bi, c, r))),
        compiler_params=_params(3, 32),
        name=f"attn_g{g}",
    )(pv, pv, pv, pv, pv)
    return o.reshape(b * s, GROUP_W), lse.reshape(b * s, V7X_LANES)


CMB_TM = 512


def _combine_kernel(o0_ref, o1_ref, o2_ref, l0_ref, l1_ref, l2_ref, att_ref):
    l0, l1, l2 = l0_ref[...], l1_ref[...], l2_ref[...]
    m = jnp.maximum(jnp.maximum(l0, l1), l2)
    e0, e1, e2 = jnp.exp(l0 - m), jnp.exp(l1 - m), jnp.exp(l2 - m)
    den = e0 + e1 + e2
    for g, (o_ref, e) in enumerate(((o0_ref, e0), (o1_ref, e1), (o2_ref, e2))):
        wgt = e / den
        for h in range(HEADS_PER_GROUP):
            sl = slice(h * HEAD_DIM, (h + 1) * HEAD_DIM)
            val = o_ref[:, sl].astype(F32) * wgt[:, h:h + 1]
            att_ref[:, g * GROUP_W + h * HEAD_DIM:g * GROUP_W + (h + 1) * HEAD_DIM] = val.astype(BF16)


def _combine(os, lses):
    t = os[0].shape[0]
    o_spec = pl.BlockSpec((CMB_TM, GROUP_W), lambda i: (i, 0))
    l_spec = pl.BlockSpec((CMB_TM, V7X_LANES), lambda i: (i, 0))
    return pl.pallas_call(
        _combine_kernel,
        out_shape=jax.ShapeDtypeStruct((t, ATT_W), BF16),
        grid=(t // CMB_TM,),
        in_specs=[o_spec] * 3 + [l_spec] * 3,
        out_specs=pl.BlockSpec((CMB_TM, ATT_W), lambda i: (i, 0)),
        compiler_params=_params(1, 32),
        name="attn_combine",
    )(*os, *lses)


LRU_TT = 512


def _lru_kernel(xr_ref, gy_ref, cw_ref, cb_ref, wr_ref, br_ref, wi_ref, bi_ref, lam_ref,
                o_ref, ext_ref, carry_ref, a_ref, b_ref):
    tt = LRU_TT
    pad = V7X_SUBLANES

    @pl.when(pl.program_id(2) == 0)
    def _():
        ext_ref[0:pad, :] = jnp.zeros((pad, LRU_BW), F32)
        carry_ref[...] = jnp.zeros((pad, LRU_BW), F32)

    ext_ref[pad:pad + tt, :] = xr_ref[...].astype(F32)
    cw = cw_ref[...]
    xc = ext_ref[pad - 3:pad - 3 + tt, :] * cw[0:1]
    for k in range(1, LRU_CONV):
        xc = xc + ext_ref[pad - 3 + k:pad - 3 + k + tt, :] * cw[k:k + 1]
    xc = xc + cb_ref[...]
    ext_ref[0:pad, :] = ext_ref[tt:tt + pad, :]

    xcb = xc.astype(BF16)
    r = _sigmoid(_dot(xcb, wr_ref[...]) + br_ref[...])
    i = _sigmoid(_dot(xcb, wi_ref[...]) + bi_ref[...])
    lam = lam_ref[...]
    softplus_neg_lam = jnp.maximum(-lam, 0.0) + jnp.log1p(jnp.exp(-jnp.abs(lam)))
    log_a = (-LRU_C * r) * softplus_neg_lam
    a = jnp.exp(log_a)
    a_ref[...] = a
    b_ref[...] = jnp.sqrt(-jnp.tanh(log_a) * (a * a + 1.0)) * i * xc

    row = lax.broadcasted_iota(jnp.int32, (pad, LRU_BW), 0)

    def body(gidx, carry):
        r0 = pl.multiple_of(gidx * pad, pad)
        av = a_ref[pl.ds(r0, pad), :]
        bv = b_ref[pl.ds(r0, pad), :]
        for s in (1, 2, 4):
            m = row >= s
            bv = jnp.where(m, av * pltpu.roll(bv, s, axis=0) + bv, bv)
            av = jnp.where(m, av * pltpu.roll(av, s, axis=0), av)
        h = av * carry + bv
        b_ref[pl.ds(r0, pad), :] = h
        return jnp.broadcast_to(h[pad - 1:pad, :], (pad, LRU_BW))

    carry_ref[...] = lax.fori_loop(0, tt // pad, body, carry_ref[...], unroll=4)
    o_ref[...] = (b_ref[...] * gy_ref[...].astype(F32)).astype(BF16)


def _rglru(proj3, cw, cb, wr_bf, br, wi_bf, bi, lam):
    b, s, _ = proj3.shape
    xr_blk, gy_blk = OFF_XR // LRU_BW, OFF_YR // LRU_BW
    vec = lambda: pl.BlockSpec((1, LRU_BW), lambda n, bi_, t: (0, n))
    out = pl.pallas_call(
        _lru_kernel,
        out_shape=jax.ShapeDtypeStruct((b, s, LRU_W), BF16),
        grid=(LRU_BLOCKS, b, s // LRU_TT),
        in_specs=[
            pl.BlockSpec((None, LRU_TT, LRU_BW), lambda n, bi_, t: (bi_, t, xr_blk + n)),
            pl.BlockSpec((None, LRU_TT, LRU_BW), lambda n, bi_, t: (bi_, t, gy_blk + n)),
            pl.BlockSpec((LRU_CONV, LRU_BW), lambda n, bi_, t: (0, n)),
            vec(),
            pl.BlockSpec((None, LRU_BW, LRU_BW), lambda n, bi_, t: (n, 0, 0)),
            vec(),
            pl.BlockSpec((None, LRU_BW, LRU_BW), lambda n, bi_, t: (n, 0, 0)),
            vec(),
            vec(),
        ],
        out_specs=pl.BlockSpec((None, LRU_TT, LRU_BW), lambda n, bi_, t: (bi_, t, n)),
        scratch_shapes=[
            pltpu.VMEM((LRU_TT + V7X_SUBLANES, LRU_BW), F32),
            pltpu.VMEM((V7X_SUBLANES, LRU_BW), F32),
            pltpu.VMEM((LRU_TT, LRU_BW), F32),
            pltpu.VMEM((LRU_TT, LRU_BW), F32),
        ],
        compiler_params=_params(3, 32),
        name="rglru",
    )(proj3, proj3, cw, cb, wr_bf, br, wi_bf, bi, lam)
    return out.reshape(b * s, LRU_W)


MRG_TM, MRG_TN = 1024, 512


def _merge_kernel(att_ref, lru_ref, wa_ref, wl_ref, sa_ref, sr_ref, o_ref):
    ya = _dot(att_ref[...], wa_ref[...])
    yl = _dot(lru_ref[...], wl_ref[...])
    o_ref[...] = (sa_ref[...].astype(F32) * ya + sr_ref[...].astype(F32) * yl).astype(BF16)


def _merge(att, lru, wa_bf, wl_bf, proj):
    t = att.shape[0]
    ga_blk, gr_blk = OFF_GA // MRG_TN, OFF_GR // MRG_TN
    return pl.pallas_call(
        _merge_kernel,
        out_shape=jax.ShapeDtypeStruct((t, D_MODEL), BF16),
        grid=(t // MRG_TM, D_MODEL // MRG_TN),
        in_specs=[
            pl.BlockSpec((MRG_TM, ATT_W), lambda i, j: (i, 0)),
            pl.BlockSpec((MRG_TM, LRU_W), lambda i, j: (i, 0)),
            pl.BlockSpec((ATT_W, MRG_TN), lambda i, j: (0, j)),
            pl.BlockSpec((LRU_W, MRG_TN), lambda i, j: (0, j)),
            pl.BlockSpec((MRG_TM, MRG_TN), lambda i, j: (i, ga_blk + j)),
            pl.BlockSpec((MRG_TM, MRG_TN), lambda i, j: (i, gr_blk + j)),
        ],
        out_specs=pl.BlockSpec((MRG_TM, MRG_TN), lambda i, j: (i, j)),
        compiler_params=_params(2, 52),
        name="merge",
    )(att, lru, wa_bf, wl_bf, proj, proj)


RES_TM, RES_TN = 1024, 512


def _res_matmul_kernel(a_ref, w_ref, x_ref, o_ref):
    o_ref[...] = x_ref[...] + _dot(a_ref[...], w_ref[...])


def _res_matmul(a, w_bf, x2, name):
    t, k = a.shape
    n = w_bf.shape[1]
    return pl.pallas_call(
        _res_matmul_kernel,
        out_shape=jax.ShapeDtypeStruct((t, n), F32),
        grid=(t // RES_TM, n // RES_TN),
        in_specs=[
            pl.BlockSpec((RES_TM, k), lambda i, j: (i, 0)),
            pl.BlockSpec((k, RES_TN), lambda i, j: (0, j)),
            pl.BlockSpec((RES_TM, RES_TN), lambda i, j: (i, j)),
        ],
        out_specs=pl.BlockSpec((RES_TM, RES_TN), lambda i, j: (i, j)),
        compiler_params=_params(2, 52),
        name=name,
    )(a, w_bf, x2)


UP_TM, UP_TN = 512, 512


def _ffn_up_kernel(seq_tiles, x_ref, g_ref, wg_ref, wv_ref, cwg_ref, cwv_ref, cbg_ref, cbv_ref,
                   o_ref, h_ref, ext_ref, halo_g_ref, halo_v_ref):
    i, j = pl.program_id(0), pl.program_id(1)
    tm = UP_TM
    pad = V7X_SUBLANES

    @pl.when(j == 0)
    def _():
        h_ref[...] = _rmsnorm(x_ref[...], g_ref[...]).astype(BF16)

    def conv(u, cw_ref, cb_ref, halo_ref):
        @pl.when(i % seq_tiles == 0)
        def _():
            ext_ref[0:pad, :] = jnp.zeros((pad, UP_TN), F32)

        @pl.when(i % seq_tiles != 0)
        def _():
            ext_ref[0:pad, :] = halo_ref[j]

        ext_ref[pad:pad + tm, :] = u
        halo_ref[j] = ext_ref[tm:tm + pad, :]
        cw = cw_ref[...]
        y = ext_ref[pad - 2:pad - 2 + tm, :] * cw[0:1]
        y = y + ext_ref[pad - 1:pad - 1 + tm, :] * cw[1:2]
        y = y + u * cw[2:3]
        return y + cb_ref[...]

    h = h_ref[...]
    gate = conv(_dot(h, wg_ref[...]), cwg_ref, cbg_ref, halo_g_ref)
    gate = _gelu_tanh(gate)
    val = conv(_dot(h, wv_ref[...]), cwv_ref, cbv_ref, halo_v_ref)
    o_ref[...] = (gate * val).astype(BF16)


def _ffn_up(x2, g, w_bf, cw, cb, seq):
    t = x2.shape[0]
    nj = FFN_DIM // UP_TN
    assert seq % UP_TM == 0
    return pl.pallas_call(
        functools.partial(_ffn_up_kernel, seq // UP_TM),
        out_shape=jax.ShapeDtypeStruct((t, FFN_DIM), BF16),
        grid=(t // UP_TM, nj),
        in_specs=[
            pl.BlockSpec((UP_TM, D_MODEL), lambda i, j: (i, 0)),
            pl.BlockSpec((1, D_MODEL), lambda i, j: (0, 0)),
            pl.BlockSpec((D_MODEL, UP_TN), lambda i, j: (0, j)),
            pl.BlockSpec((D_MODEL, UP_TN), lambda i, j: (0, nj + j)),
            pl.BlockSpec((FFN_CONV, UP_TN), lambda i, j: (0, j)),
            pl.BlockSpec((FFN_CONV, UP_TN), lambda i, j: (0, nj + j)),
            pl.BlockSpec((1, UP_TN), lambda i, j: (0, j)),
            pl.BlockSpec((1, UP_TN), lambda i, j: (0, nj + j)),
        ],
        out_specs=pl.BlockSpec((UP_TM, UP_TN), lambda i, j: (i, j)),
        scratch_shapes=[
            pltpu.VMEM((UP_TM, D_MODEL), BF16),
            pltpu.VMEM((UP_TM + V7X_SUBLANES, UP_TN), F32),
            pltpu.VMEM((nj, V7X_SUBLANES, UP_TN), F32),
            pltpu.VMEM((nj, V7X_SUBLANES, UP_TN), F32),
        ],
        compiler_params=_params(2, 56),
        name="ffn_up",
    )(x2, g, w_bf, w_bf, cw, cw, cb, cb)


DN_TM, DN_TN, DN_TK = 1024, 1024, 2048


def _ffn_down_kernel(a_ref, w_ref, x_ref, o_ref, acc_ref):
    k = pl.program_id(2)

    @pl.when(k == 0)
    def _():
        acc_ref[...] = x_ref[...]

    acc_ref[...] += _dot(a_ref[...], w_ref[...])

    @pl.when(k == pl.num_programs(2) - 1)
    def _():
        o_ref[...] = acc_ref[...]


def _ffn_down(a, w_bf, x2):
    t, kk = a.shape
    n = w_bf.shape[1]
    return pl.pallas_call(
        _ffn_down_kernel,
        out_shape=jax.ShapeDtypeStruct((t, n), F32),
        grid=(t // DN_TM, n // DN_TN, kk // DN_TK),
        in_specs=[
            pl.BlockSpec((DN_TM, DN_TK), lambda i, j, k: (i, k)),
            pl.BlockSpec((DN_TK, DN_TN), lambda i, j, k: (k, j)),
            pl.BlockSpec((DN_TM, DN_TN), lambda i, j, k: (i, j)),
        ],
        out_specs=pl.BlockSpec((DN_TM, DN_TN), lambda i, j, k: (i, j)),
        scratch_shapes=[pltpu.VMEM((DN_TM, DN_TN), F32)],
        compiler_params=_params(3, 48),
        name="ffn_down",
    )(a, w_bf, x2)


PLE_TM, PLE_TN = 256, 1024


def _ple_kernel(x_ref, p_ref, gg_ref, wg_ref, wp_ref, ge_ref, gf_ref, o_ref, h_ref, e_ref):
    j = pl.program_id(1)

    @pl.when(j == 0)
    def _():
        h_ref[...] = _rmsnorm(x_ref[...], gg_ref[...]).astype(BF16)
        e_ref[...] = _rmsnorm(_dot(p_ref[...], wp_ref[...]), ge_ref[...])

    col = pl.multiple_of(j * PLE_TN, PLE_TN)
    gate = _sigmoid(_dot(h_ref[...], wg_ref[...]))
    e_ref[:, pl.ds(col, PLE_TN)] = x_ref[:, pl.ds(col, PLE_TN)] + gate * e_ref[:, pl.ds(col, PLE_TN)]

    @pl.when(j == pl.num_programs(1) - 1)
    def _():
        o_ref[...] = _rmsnorm(e_ref[...], gf_ref[...])


def _ple(x2, p_bf, gg, wg_bf, wp_bf, ge, gf):
    t = x2.shape[0]
    row = lambda: pl.BlockSpec((1, D_MODEL), lambda i, j: (0, 0))
    return pl.pallas_call(
        _ple_kernel,
        out_shape=jax.ShapeDtypeStruct((t, D_MODEL), F32),
        grid=(t // PLE_TM, D_MODEL // PLE_TN),
        in_specs=[
            pl.BlockSpec((PLE_TM, D_MODEL), lambda i, j: (i, 0)),
            pl.BlockSpec((PLE_TM, PLE_DIM), lambda i, j: (i, 0)),
            row(),
            pl.BlockSpec((D_MODEL, PLE_TN), lambda i, j: (0, j)),
            pl.BlockSpec((PLE_DIM, D_MODEL), lambda i, j: (0, 0)),
            row(),
            row(),
        ],
        out_specs=pl.BlockSpec((PLE_TM, D_MODEL), lambda i, j: (i, 0)),
        scratch_shapes=[
            pltpu.VMEM((PLE_TM, D_MODEL), BF16),
            pltpu.VMEM((PLE_TM, D_MODEL), F32),
        ],
        compiler_params=_params(2, 56),
        name="ple_final",
    )(x2, p_bf, gg, wg_bf, wp_bf, ge, gf)


def kernel(x, p, mix_norm_g, w_in, lru_conv_w, lru_conv_b, w_rgate, b_rgate, w_igate, b_igate, lru_lambda, w_att_branch, w_lru_branch, w_out, ffn_norm_g, w_up, ffn_conv_w, ffn_conv_b, w_down, ple_proj, ple_norm_g, ple_gate_norm_g, ple_gate_w, final_norm_g):
    b, s, d = x.shape
    depth = w_in.shape[0]
    t = b * s
    row = lambda v: v.reshape(1, -1)
    xf = x.reshape(t, d)
    for l in range(depth):
        proj = _in_proj(xf, row(mix_norm_g[l]), w_in[l].astype(BF16))
        proj3 = proj.reshape(b, s, IN_COLS)
        os, lses = zip(*[_attention_group(proj3, g, dil) for g, (_, dil) in enumerate(PATTERNS)])
        att = _combine(os, lses)
        lru = _rglru(proj3, lru_conv_w[l], row(lru_conv_b[l]), w_rgate[l].astype(BF16), row(b_rgate[l]),
                     w_igate[l].astype(BF16), row(b_igate[l]), row(lru_lambda[l]))
        merged = _merge(att, lru, w_att_branch[l].astype(BF16), w_lru_branch[l].astype(BF16), proj)
        xf = _res_matmul(merged, w_out[l].astype(BF16), xf, "out_proj")
        gact = _ffn_up(xf, row(ffn_norm_g[l]), w_up[l].astype(BF16), ffn_conv_w[l], row(ffn_conv_b[l]), s)
        xf = _ffn_down(gact, w_down[l].astype(BF16), xf)
        assert l == depth - 1 == 0
        xf = _ple(xf, p[l].reshape(t, PLE_DIM).astype(BF16), row(ple_gate_norm_g[l]),
                  ple_gate_w[l].astype(BF16), ple_proj[l].astype(BF16), row(ple_norm_g[l]), row(final_norm_g))
    return xf.reshape(b, s, d)
```

```python
import functools
import math

import jax
import jax.numpy as jnp
from jax import lax
from jax.experimental import pallas as pl
from jax.experimental.pallas import tpu as pltpu

F32 = jnp.float32
BF16 = jnp.bfloat16

D_MODEL = 4096
HEAD_DIM = 128
HEADS_PER_GROUP = 8
PATTERNS = ((128, 1), (512, 4), (2048, 16))
N_GROUPS = len(PATTERNS)
GROUP_W = HEADS_PER_GROUP * HEAD_DIM
ATT_W = N_GROUPS * GROUP_W
ATT_BLK = 128
LRU_W = D_MODEL
LRU_BLOCKS = 16
LRU_BW = LRU_W // LRU_BLOCKS
LRU_C = 8.0
LRU_CONV = 4
FFN_DIM = 3 * D_MODEL
FFN_CONV = 3
PLE_DIM = 256
EPS = 1e-6
OFF_Q, OFF_K, OFF_V = 0, ATT_W, 2 * ATT_W
OFF_XR = 3 * ATT_W
OFF_YR = OFF_XR + LRU_W
OFF_GA = OFF_YR + LRU_W
OFF_GR = OFF_GA + D_MODEL
IN_COLS = OFF_GR + D_MODEL

V7X_VMEM_BYTES = 64 * 1024 * 1024
V7X_SUBLANES = 8
V7X_LANES = 128
MIB = 1024 * 1024


def _params(n_axes, vmem_mib):
    assert vmem_mib * MIB < V7X_VMEM_BYTES
    return pltpu.CompilerParams(
        dimension_semantics=("arbitrary",) * n_axes,
        vmem_limit_bytes=vmem_mib * MIB,
    )


def _sigmoid(v):
    return 1.0 / (1.0 + jnp.exp(-v))


def _gelu_tanh(v):
    c = math.sqrt(2.0 / math.pi)
    return v * (0.5 * (1.0 + jnp.tanh(c * (v + 0.044715 * (v * v * v)))))


def _rmsnorm(v, g):
    return v * lax.rsqrt(jnp.mean(v * v, axis=-1, keepdims=True) + EPS) * g


def _dot(a, b):
    return jnp.dot(a, b, preferred_element_type=F32)


IN_TM, IN_TN, IN_SUB = 512, 1024, 256


def _in_proj_kernel(x_ref, g_ref, w_ref, o_ref, h_ref):
    j = pl.program_id(1)

    @pl.when(j == 0)
    def _():
        h_ref[...] = _rmsnorm(x_ref[...], g_ref[...]).astype(BF16)

    def body(act):
        h = h_ref[...]
        for s in range(IN_TN // IN_SUB):
            sl = slice(s * IN_SUB, (s + 1) * IN_SUB)
            o_ref[:, sl] = act(_dot(h, w_ref[:, sl])).astype(BF16)

    j_yr, j_ga = OFF_YR // IN_TN, OFF_GA // IN_TN
    pl.when(j < j_yr)(functools.partial(body, lambda v: v))
    pl.when((j >= j_yr) & (j < j_ga))(functools.partial(body, _gelu_tanh))
    pl.when(j >= j_ga)(functools.partial(body, _sigmoid))


def _in_proj(x2, g, w_bf):
    t = x2.shape[0]
    assert OFF_YR % IN_TN == 0 and OFF_GA % IN_TN == 0
    return pl.pallas_call(
        _in_proj_kernel,
        out_shape=jax.ShapeDtypeStruct((t, IN_COLS), BF16),
        grid=(t // IN_TM, IN_COLS // IN_TN),
        in_specs=[
            pl.BlockSpec((IN_TM, D_MODEL), lambda i, j: (i, 0)),
            pl.BlockSpec((1, D_MODEL), lambda i, j: (0, 0)),
            pl.BlockSpec((D_MODEL, IN_TN), lambda i, j: (0, j)),
        ],
        out_specs=pl.BlockSpec((IN_TM, IN_TN), lambda i, j: (i, j)),
        scratch_shapes=[pltpu.VMEM((IN_TM, D_MODEL), BF16)],
        compiler_params=_params(2, 56),
        name="in_proj",
    )(x2, g, w_bf)


def _attn_kernel(q_ref, kp_ref, kc_ref, vp_ref, vc_ref, o_ref, lse_ref):
    c = pl.program_id(2)
    qi = lax.broadcasted_iota(jnp.int32, (ATT_BLK, ATT_BLK), 0)
    kj = lax.broadcasted_iota(jnp.int32, (ATT_BLK, ATT_BLK), 1)
    mask_c = kj <= qi
    mask_p = (kj >= qi) & (c > 0)
    neg = jnp.finfo(F32).min
    scale = HEAD_DIM ** -0.5
    nt = (((1,), (1,)), ((), ()))
    lse_tile = jnp.zeros((ATT_BLK, V7X_LANES), F32)
    for h in range(HEADS_PER_GROUP):
        sl = slice(h * HEAD_DIM, (h + 1) * HEAD_DIM)
        q = q_ref[:, sl]
        s_c = lax.dot_general(q, kc_ref[:, sl], nt, preferred_element_type=F32) * scale
        s_p = lax.dot_general(q, kp_ref[:, sl], nt, preferred_element_type=F32) * scale
        s_c = jnp.where(mask_c, s_c, neg)
        s_p = jnp.where(mask_p, s_p, neg)
        mx = jnp.maximum(jnp.max(s_c, axis=-1, keepdims=True), jnp.max(s_p, axis=-1, keepdims=True))
        p_c = jnp.exp(s_c - mx)
        p_p = jnp.exp(s_p - mx)
        den = jnp.sum(p_c, axis=-1, keepdims=True) + jnp.sum(p_p, axis=-1, keepdims=True)
        o = (_dot(p_c.astype(BF16), vc_ref[:, sl]) + _dot(p_p.astype(BF16), vp_ref[:, sl])) / den
        o_ref[:, sl] = o.astype(BF16)
        lse = mx + jnp.log(den)
        lse_tile = jnp.where(kj == h, lse, lse_tile)
    lse_ref[...] = lse_tile


def _attention_group(proj3, g, dil):
    b, s, _ = proj3.shape
    l = s // dil
    assert l % ATT_BLK == 0
    nb = l // ATT_BLK
    if dil == 1:
        pv, cb, base = proj3, IN_COLS // GROUP_W, 3 * g
    else:
        pv = proj3[:, :, 3 * g * GROUP_W:3 * (g + 1) * GROUP_W].reshape(b, l, dil * 3 * GROUP_W)
        cb, base = 3, 0

    def cur(which):
        return pl.BlockSpec((None, ATT_BLK, GROUP_W), lambda bi, r, c: (bi, c, r * cb + base + which))

    def prev(which):
        return pl.BlockSpec((None, ATT_BLK, GROUP_W),
                            lambda bi, r, c: (bi, jnp.maximum(c - 1, 0), r * cb + base + which))

    o, lse = pl.pallas_call(
        _attn_kernel,
        out_shape=(jax.ShapeDtypeStruct((b, l, dil * GROUP_W), BF16),
                   jax.ShapeDtypeStruct((b, l, dil * V7X_LANES), F32)),
        grid=(b, dil, nb),
        in_specs=[cur(0), prev(1), cur(1), prev(2), cur(2)],
        out_specs=(pl.BlockSpec((None, ATT_BLK, GROUP_W), lambda bi, r, c: (bi, c, r)),
                   pl.BlockSpec((None, ATT_BLK, V7X_LANES), lambda bi, r, c: (bi, c, r))),
        compiler_params=_params(3, 32),
        name=f"attn_g{g}",
    )(pv, pv, pv, pv, pv)
    return o.reshape(b * s, GROUP_W), lse.reshape(b * s, V7X_LANES)


CMB_TM = 512


def _combine_kernel(o0_ref, o1_ref, o2_ref, l0_ref, l1_ref, l2_ref, att_ref):
    l0, l1, l2 = l0_ref[...], l1_ref[...], l2_ref[...]
    m = jnp.maximum(jnp.maximum(l0, l1), l2)
    e0, e1, e2 = jnp.exp(l0 - m), jnp.exp(l1 - m), jnp.exp(l2 - m)
    den = e0 + e1 + e2
    for g, (o_ref, e) in enumerate(((o0_ref, e0), (o1_ref, e1), (o2_ref, e2))):
        wgt = e / den
        for h in range(HEADS_PER_GROUP):
            sl = slice(h * HEAD_DIM, (h + 1) * HEAD_DIM)
            val = o_ref[:, sl].astype(F32) * wgt[:, h:h + 1]
            att_ref[:, g * GROUP_W + h * HEAD_DIM:g * GROUP_W + (h + 1) * HEAD_DIM] = val.astype(BF16)


def _combine(os, lses):
    t = os[0].shape[0]
    o_spec = pl.BlockSpec((CMB_TM, GROUP_W), lambda i: (i, 0))
    l_spec = pl.BlockSpec((CMB_TM, V7X_LANES), lambda i: (i, 0))
    return pl.pallas_call(
        _combine_kernel,
        out_shape=jax.ShapeDtypeStruct((t, ATT_W), BF16),
        grid=(t // CMB_TM,),
        in_specs=[o_spec] * 3 + [l_spec] * 3,
        out_specs=pl.BlockSpec((CMB_TM, ATT_W), lambda i: (i, 0)),
        compiler_params=_params(1, 32),
        name="attn_combine",
    )(*os, *lses)


LRU_TT = 512


def _lru_kernel(xr_ref, gy_ref, cw_ref, cb_ref, wr_ref, br_ref, wi_ref, bi_ref, lam_ref,
                o_ref, ext_ref, carry_ref, a_ref, b_ref):
    tt = LRU_TT
    pad = V7X_SUBLANES

    @pl.when(pl.program_id(2) == 0)
    def _():
        ext_ref[0:pad, :] = jnp.zeros((pad, LRU_BW), F32)
        carry_ref[...] = jnp.zeros((pad, LRU_BW), F32)

    ext_ref[pad:pad + tt, :] = xr_ref[...].astype(F32)
    cw = cw_ref[...]
    xc = ext_ref[pad - 3:pad - 3 + tt, :] * cw[0:1]
    for k in range(1, LRU_CONV):
        xc = xc + ext_ref[pad - 3 + k:pad - 3 + k + tt, :] * cw[k:k + 1]
    xc = xc + cb_ref[...]
    ext_ref[0:pad, :] = ext_ref[tt:tt + pad, :]

    xcb = xc.astype(BF16)
    r = _sigmoid(_dot(xcb, wr_ref[...]) + br_ref[...])
    i = _sigmoid(_dot(xcb, wi_ref[...]) + bi_ref[...])
    lam = lam_ref[...]
    softplus_neg_lam = jnp.maximum(-lam, 0.0) + jnp.log1p(jnp.exp(-jnp.abs(lam)))
    log_a = (-LRU_C * r) * softplus_neg_lam
    a = jnp.exp(log_a)
    a_ref[...] = a
    b_ref[...] = jnp.sqrt(-jnp.tanh(log_a) * (a * a + 1.0)) * i * xc

    row = lax.broadcasted_iota(jnp.int32, (pad, LRU_BW), 0)

    def body(gidx, carry):
        r0 = pl.multiple_of(gidx * pad, pad)
        av = a_ref[pl.ds(r0, pad), :]
        bv = b_ref[pl.ds(r0, pad), :]
        for s in (1, 2, 4):
            m = row >= s
            bv = jnp.where(m, av * pltpu.roll(bv, s, axis=0) + bv, bv)
            av = jnp.where(m, av * pltpu.roll(av, s, axis=0), av)
        h = av * carry + bv
        b_ref[pl.ds(r0, pad), :] = h
        return jnp.broadcast_to(h[pad - 1:pad, :], (pad, LRU_BW))

    carry_ref[...] = lax.fori_loop(0, tt // pad, body, carry_ref[...], unroll=4)
    o_ref[...] = (b_ref[...] * gy_ref[...].astype(F32)).astype(BF16)


def _rglru(proj3, cw, cb, wr_bf, br, wi_bf, bi, lam):
    b, s, _ = proj3.shape
    xr_blk, gy_blk = OFF_XR // LRU_BW, OFF_YR // LRU_BW
    vec = lambda: pl.BlockSpec((1, LRU_BW), lambda n, bi_, t: (0, n))
    out = pl.pallas_call(
        _lru_kernel,
        out_shape=jax.ShapeDtypeStruct((b, s, LRU_W), BF16),
        grid=(LRU_BLOCKS, b, s // LRU_TT),
        in_specs=[
            pl.BlockSpec((None, LRU_TT, LRU_BW), lambda n, bi_, t: (bi_, t, xr_blk + n)),
            pl.BlockSpec((None, LRU_TT, LRU_BW), lambda n, bi_, t: (bi_, t, gy_blk + n)),
            pl.BlockSpec((LRU_CONV, LRU_BW), lambda n, bi_, t: (0, n)),
            vec(),
            pl.BlockSpec((None, LRU_BW, LRU_BW), lambda n, bi_, t: (n, 0, 0)),
            vec(),
            pl.BlockSpec((None, LRU_BW, LRU_BW), lambda n, bi_, t: (n, 0, 0)),
            vec(),
            vec(),
        ],
        out_specs=pl.BlockSpec((None, LRU_TT, LRU_BW), lambda n, bi_, t: (bi_, t, n)),
        scratch_shapes=[
            pltpu.VMEM((LRU_TT + V7X_SUBLANES, LRU_BW), F32),
            pltpu.VMEM((V7X_SUBLANES, LRU_BW), F32),
            pltpu.VMEM((LRU_TT, LRU_BW), F32),
            pltpu.VMEM((LRU_TT, LRU_BW), F32),
        ],
        compiler_params=_params(3, 32),
        name="rglru",
    )(proj3, proj3, cw, cb, wr_bf, br, wi_bf, bi, lam)
    return out.reshape(b * s, LRU_W)


MRG_TM, MRG_TN = 1024, 512


def _merge_kernel(att_ref, lru_ref, wa_ref, wl_ref, sa_ref, sr_ref, o_ref):
    ya = _dot(att_ref[...], wa_ref[...])
    yl = _dot(lru_ref[...], wl_ref[...])
    o_ref[...] = (sa_ref[...].astype(F32) * ya + sr_ref[...].astype(F32) * yl).astype(BF16)


def _merge(att, lru, wa_bf, wl_bf, proj):
    t = att.shape[0]
    ga_blk, gr_blk = OFF_GA // MRG_TN, OFF_GR // MRG_TN
    return pl.pallas_call(
        _merge_kernel,
        out_shape=jax.ShapeDtypeStruct((t, D_MODEL), BF16),
        grid=(t // MRG_TM, D_MODEL // MRG_TN),
        in_specs=[
            pl.BlockSpec((MRG_TM, ATT_W), lambda i, j: (i, 0)),
            pl.BlockSpec((MRG_TM, LRU_W), lambda i, j: (i, 0)),
            pl.BlockSpec((ATT_W, MRG_TN), lambda i, j: (0, j)),
            pl.BlockSpec((LRU_W, MRG_TN), lambda i, j: (0, j)),
            pl.BlockSpec((MRG_TM, MRG_TN), lambda i, j: (i, ga_blk + j)),
            pl.BlockSpec((MRG_TM, MRG_TN), lambda i, j: (i, gr_blk + j)),
        ],
        out_specs=pl.BlockSpec((MRG_TM, MRG_TN), lambda i, j: (i, j)),
        compiler_params=_params(2, 52),
        name="merge",
    )(att, lru, wa_bf, wl_bf, proj, proj)


RES_TM, RES_TN = 1024, 512


def _res_matmul_kernel(a_ref, w_ref, x_ref, o_ref):
    o_ref[...] = x_ref[...] + _dot(a_ref[...], w_ref[...])


def _res_matmul(a, w_bf, x2, name):
    t, k = a.shape
    n = w_bf.shape[1]
    return pl.pallas_call(
        _res_matmul_kernel,
        out_shape=jax.ShapeDtypeStruct((t, n), F32),
        grid=(t // RES_TM, n // RES_TN),
        in_specs=[
            pl.BlockSpec((RES_TM, k), lambda i, j: (i, 0)),
            pl.BlockSpec((k, RES_TN), lambda i, j: (0, j)),
            pl.BlockSpec((RES_TM, RES_TN), lambda i, j: (i, j)),
        ],
        out_specs=pl.BlockSpec((RES_TM, RES_TN), lambda i, j: (i, j)),
        compiler_params=_params(2, 52),
        name=name,
    )(a, w_bf, x2)


UP_TM, UP_TN, UP_SUB, UP_ROWS = 512, 512, 256, 16


def _ffn_up_kernel(seq_tiles, x_ref, g_ref, wg_ref, wv_ref, cwg_ref, cwv_ref, cbg_ref, cbv_ref,
                   o_ref, h_ref, ext_g_ref, ext_v_ref, halo_g_ref, halo_v_ref):
    i, j = pl.program_id(0), pl.program_id(1)
    tm = UP_TM
    pad = V7X_SUBLANES

    @pl.when(j == 0)
    def _():
        h_ref[...] = _rmsnorm(x_ref[...], g_ref[...]).astype(BF16)

    @pl.when((i == 0) & (j == 0))
    def _():
        halo_g_ref[...] = jnp.zeros(halo_g_ref.shape, F32)
        halo_v_ref[...] = jnp.zeros(halo_v_ref.shape, F32)

    seq_start = i % seq_tiles == 0
    h = h_ref[...]

    def raw(w_ref, ext_ref, halo_ref, sl):
        ext_ref[pad:pad + tm, sl] = _dot(h, w_ref[:, sl])
        ext_ref[0:pad, sl] = jnp.where(seq_start, 0.0, halo_ref[j, :, sl])
        halo_ref[j, :, sl] = ext_ref[tm:tm + pad, sl]

    def conv(ext_ref, cw, cb, r0, sl):
        y = ext_ref[pad - 2 + r0:pad - 2 + r0 + UP_ROWS, sl] * cw[0:1]
        y = y + ext_ref[pad - 1 + r0:pad - 1 + r0 + UP_ROWS, sl] * cw[1:2]
        y = y + ext_ref[pad + r0:pad + r0 + UP_ROWS, sl] * cw[2:3]
        return y + cb

    for s in range(UP_TN // UP_SUB):
        sl = slice(s * UP_SUB, (s + 1) * UP_SUB)
        raw(wg_ref, ext_g_ref, halo_g_ref, sl)
        raw(wv_ref, ext_v_ref, halo_v_ref, sl)
        cwg, cwv, cbg, cbv = cwg_ref[:, sl], cwv_ref[:, sl], cbg_ref[:, sl], cbv_ref[:, sl]
        for r0 in range(0, tm, UP_ROWS):
            gate = _gelu_tanh(conv(ext_g_ref, cwg, cbg, r0, sl))
            val = conv(ext_v_ref, cwv, cbv, r0, sl)
            o_ref[r0:r0 + UP_ROWS, sl] = (gate * val).astype(BF16)


def _ffn_up(x2, g, w_bf, cw, cb, seq):
    t = x2.shape[0]
    nj = FFN_DIM // UP_TN
    assert seq % UP_TM == 0
    return pl.pallas_call(
        functools.partial(_ffn_up_kernel, seq // UP_TM),
        out_shape=jax.ShapeDtypeStruct((t, FFN_DIM), BF16),
        grid=(t // UP_TM, nj),
        in_specs=[
            pl.BlockSpec((UP_TM, D_MODEL), lambda i, j: (i, 0)),
            pl.BlockSpec((1, D_MODEL), lambda i, j: (0, 0)),
            pl.BlockSpec((D_MODEL, UP_TN), lambda i, j: (0, j)),
            pl.BlockSpec((D_MODEL, UP_TN), lambda i, j: (0, nj + j)),
            pl.BlockSpec((FFN_CONV, UP_TN), lambda i, j: (0, j)),
            pl.BlockSpec((FFN_CONV, UP_TN), lambda i, j: (0, nj + j)),
            pl.BlockSpec((1, UP_TN), lambda i, j: (0, j)),
            pl.BlockSpec((1, UP_TN), lambda i, j: (0, nj + j)),
        ],
        out_specs=pl.BlockSpec((UP_TM, UP_TN), lambda i, j: (i, j)),
        scratch_shapes=[
            pltpu.VMEM((UP_TM, D_MODEL), BF16),
            pltpu.VMEM((UP_TM + V7X_SUBLANES, UP_TN), F32),
            pltpu.VMEM((UP_TM + V7X_SUBLANES, UP_TN), F32),
            pltpu.VMEM((nj, V7X_SUBLANES, UP_TN), F32),
            pltpu.VMEM((nj, V7X_SUBLANES, UP_TN), F32),
        ],
        compiler_params=_params(2, 56),
        name="ffn_up",
    )(x2, g, w_bf, w_bf, cw, cw, cb, cb)


DN_TM, DN_TN, DN_TK = 1024, 1024, 2048


def _ffn_down_kernel(a_ref, w_ref, x_ref, o_ref, acc_ref):
    k = pl.program_id(2)

    @pl.when(k == 0)
    def _():
        acc_ref[...] = x_ref[...]

    acc_ref[...] += _dot(a_ref[...], w_ref[...])

    @pl.when(k == pl.num_programs(2) - 1)
    def _():
        o_ref[...] = acc_ref[...]


def _ffn_down(a, w_bf, x2):
    t, kk = a.shape
    n = w_bf.shape[1]
    return pl.pallas_call(
        _ffn_down_kernel,
        out_shape=jax.ShapeDtypeStruct((t, n), F32),
        grid=(t // DN_TM, n // DN_TN, kk // DN_TK),
        in_specs=[
            pl.BlockSpec((DN_TM, DN_TK), lambda i, j, k: (i, k)),
            pl.BlockSpec((DN_TK, DN_TN), lambda i, j, k: (k, j)),
            pl.BlockSpec((DN_TM, DN_TN), lambda i, j, k: (i, j)),
        ],
        out_specs=pl.BlockSpec((DN_TM, DN_TN), lambda i, j, k: (i, j)),
        scratch_shapes=[pltpu.VMEM((DN_TM, DN_TN), F32)],
        compiler_params=_params(3, 48),
        name="ffn_down",
    )(a, w_bf, x2)


PLE_TM, PLE_TN = 512, 512


def _ple_kernel(x_ref, p_ref, gg_ref, wg_ref, wp_ref, ge_ref, gf_ref, o_ref, h_ref):
    j = pl.program_id(1)

    @pl.when(j == 0)
    def _():
        h_ref[...] = _rmsnorm(x_ref[...], gg_ref[...]).astype(BF16)
        for c in range(0, D_MODEL, PLE_TN):
            o_ref[:, c:c + PLE_TN] = _dot(p_ref[...], wp_ref[:, c:c + PLE_TN])
        o_ref[...] = _rmsnorm(o_ref[...], ge_ref[...])

    col = pl.multiple_of(j * PLE_TN, PLE_TN)
    gate = _sigmoid(_dot(h_ref[...], wg_ref[...]))
    o_ref[:, pl.ds(col, PLE_TN)] = x_ref[:, pl.ds(col, PLE_TN)] + gate * o_ref[:, pl.ds(col, PLE_TN)]

    @pl.when(j == pl.num_programs(1) - 1)
    def _():
        o_ref[...] = _rmsnorm(o_ref[...], gf_ref[...])


def _ple(x2, p_bf, gg, wg_bf, wp_bf, ge, gf):
    t = x2.shape[0]
    row = lambda: pl.BlockSpec((1, D_MODEL), lambda i, j: (0, 0))
    return pl.pallas_call(
        _ple_kernel,
        out_shape=jax.ShapeDtypeStruct((t, D_MODEL), F32),
        grid=(t // PLE_TM, D_MODEL // PLE_TN),
        in_specs=[
            pl.BlockSpec((PLE_TM, D_MODEL), lambda i, j: (i, 0)),
            pl.BlockSpec((PLE_TM, PLE_DIM), lambda i, j: (i, 0)),
            row(),
            pl.BlockSpec((D_MODEL, PLE_TN), lambda i, j: (0, j)),
            pl.BlockSpec((PLE_DIM, D_MODEL), lambda i, j: (0, 0)),
            row(),
            row(),
        ],
        out_specs=pl.BlockSpec((PLE_TM, D_MODEL), lambda i, j: (i, 0)),
        scratch_shapes=[pltpu.VMEM((PLE_TM, D_MODEL), BF16)],
        compiler_params=_params(2, 56),
        name="ple_final",
    )(x2, p_bf, gg, wg_bf, wp_bf, ge, gf)


def _group_major_qkv(w):
    cols = [w[:, base + g * GROUP_W:base + (g + 1) * GROUP_W]
            for g in range(N_GROUPS) for base in (0, ATT_W, 2 * ATT_W)]
    return jnp.concatenate(cols + [w[:, OFF_XR:]], axis=1)


def kernel(x, p, mix_norm_g, w_in, lru_conv_w, lru_conv_b, w_rgate, b_rgate, w_igate, b_igate, lru_lambda, w_att_branch, w_lru_branch, w_out, ffn_norm_g, w_up, ffn_conv_w, ffn_conv_b, w_down, ple_proj, ple_norm_g, ple_gate_norm_g, ple_gate_w, final_norm_g):
    b, s, d = x.shape
    depth = w_in.shape[0]
    t = b * s
    row = lambda v: v.reshape(1, -1)
    xf = x.reshape(t, d)
    for l in range(depth):
        proj = _in_proj(xf, row(mix_norm_g[l]), _group_major_qkv(w_in[l]).astype(BF16))
        proj3 = proj.reshape(b, s, IN_COLS)
        os, lses = zip(*[_attention_group(proj3, g, dil) for g, (_, dil) in enumerate(PATTERNS)])
        att = _combine(os, lses)
        lru = _rglru(proj3, lru_conv_w[l], row(lru_conv_b[l]), w_rgate[l].astype(BF16), row(b_rgate[l]),
                     w_igate[l].astype(BF16), row(b_igate[l]), row(lru_lambda[l]))
        merged = _merge(att, lru, w_att_branch[l].astype(BF16), w_lru_branch[l].astype(BF16), proj)
        xf = _res_matmul(merged, w_out[l].astype(BF16), xf, "out_proj")
        gact = _ffn_up(xf, row(ffn_norm_g[l]), w_up[l].astype(BF16), ffn_conv_w[l], row(ffn_conv_b[l]), s)
        xf = _ffn_down(gact, w_down[l].astype(BF16), xf)
        assert l == depth - 1 == 0
        xf = _ple(xf, p[l].reshape(t, PLE_DIM).astype(BF16), row(ple_gate_norm_g[l]),
                  ple_gate_w[l].astype(BF16), ple_proj[l].astype(BF16), row(ple_norm_g[l]), row(final_norm_g))
    return xf.reshape(b, s, d)
```

```python
import functools
import math

import jax
import jax.numpy as jnp
from jax import lax
from jax.experimental import pallas as pl
from jax.experimental.pallas import tpu as pltpu

F32 = jnp.float32
BF16 = jnp.bfloat16

D_MODEL = 4096
HEAD_DIM = 128
HEADS_PER_GROUP = 8
PATTERNS = ((128, 1), (512, 4), (2048, 16))
N_GROUPS = len(PATTERNS)
GROUP_W = HEADS_PER_GROUP * HEAD_DIM
ATT_W = N_GROUPS * GROUP_W
ATT_BLK = 128
LRU_W = D_MODEL
LRU_BLOCKS = 16
LRU_BW = LRU_W // LRU_BLOCKS
LRU_C = 8.0
LRU_CONV = 4
FFN_DIM = 3 * D_MODEL
FFN_CONV = 3
PLE_DIM = 256
EPS = 1e-6
OFF_Q, OFF_K, OFF_V = 0, ATT_W, 2 * ATT_W
OFF_XR = 3 * ATT_W
OFF_YR = OFF_XR + LRU_W
OFF_GA = OFF_YR + LRU_W
OFF_GR = OFF_GA + D_MODEL
IN_COLS = OFF_GR + D_MODEL

V7X_VMEM_BYTES = 64 * 1024 * 1024
V7X_SUBLANES = 8
V7X_LANES = 128
MIB = 1024 * 1024


def _params(n_axes, vmem_mib):
    assert vmem_mib * MIB < V7X_VMEM_BYTES
    return pltpu.CompilerParams(
        dimension_semantics=("arbitrary",) * n_axes,
        vmem_limit_bytes=vmem_mib * MIB,
    )


def _sigmoid(v):
    return 1.0 / (1.0 + jnp.exp(-v))


def _gelu_tanh(v):
    c = math.sqrt(2.0 / math.pi)
    return v * (0.5 * (1.0 + jnp.tanh(c * (v + 0.044715 * (v * v * v)))))


def _rmsnorm(v, g):
    return v * lax.rsqrt(jnp.mean(v * v, axis=-1, keepdims=True) + EPS) * g


def _dot(a, b):
    return jnp.dot(a, b, preferred_element_type=F32)


IN_TM, IN_TN, IN_SUB = 512, 1024, 256


def _in_proj_kernel(x_ref, g_ref, w_ref, o_ref, h_ref):
    j = pl.program_id(1)

    @pl.when(j == 0)
    def _():
        h_ref[...] = _rmsnorm(x_ref[...], g_ref[...]).astype(BF16)

    def body(act):
        h = h_ref[...]
        for s in range(IN_TN // IN_SUB):
            sl = slice(s * IN_SUB, (s + 1) * IN_SUB)
            o_ref[:, sl] = act(_dot(h, w_ref[:, sl])).astype(BF16)

    j_yr, j_ga = OFF_YR // IN_TN, OFF_GA // IN_TN
    pl.when(j < j_yr)(functools.partial(body, lambda v: v))
    pl.when((j >= j_yr) & (j < j_ga))(functools.partial(body, _gelu_tanh))
    pl.when(j >= j_ga)(functools.partial(body, _sigmoid))


def _in_proj(x2, g, w_bf):
    t = x2.shape[0]
    assert OFF_YR % IN_TN == 0 and OFF_GA % IN_TN == 0
    return pl.pallas_call(
        _in_proj_kernel,
        out_shape=jax.ShapeDtypeStruct((t, IN_COLS), BF16),
        grid=(t // IN_TM, IN_COLS // IN_TN),
        in_specs=[
            pl.BlockSpec((IN_TM, D_MODEL), lambda i, j: (i, 0)),
            pl.BlockSpec((1, D_MODEL), lambda i, j: (0, 0)),
            pl.BlockSpec((D_MODEL, IN_TN), lambda i, j: (0, j)),
        ],
        out_specs=pl.BlockSpec((IN_TM, IN_TN), lambda i, j: (i, j)),
        scratch_shapes=[pltpu.VMEM((IN_TM, D_MODEL), BF16)],
        compiler_params=_params(2, 56),
        name="in_proj",
    )(x2, g, w_bf)


def _attn_kernel(q_ref, kp_ref, kc_ref, vp_ref, vc_ref, o_ref, lse_ref, sc_ref, sp_ref, pc_ref, pp_ref):
    c = pl.program_id(2)
    qi = lax.broadcasted_iota(jnp.int32, (ATT_BLK, ATT_BLK), 0)
    kj = lax.broadcasted_iota(jnp.int32, (ATT_BLK, ATT_BLK), 1)
    mask_c = kj <= qi
    mask_p = (kj >= qi) & (c > 0)
    neg = jnp.finfo(F32).min
    scale = HEAD_DIM ** -0.5
    nt = (((1,), (1,)), ((), ()))
    heads = [slice(h * HEAD_DIM, (h + 1) * HEAD_DIM) for h in range(HEADS_PER_GROUP)]
    for h, sl in enumerate(heads):
        q = q_ref[:, sl]
        sc_ref[h] = lax.dot_general(q, kc_ref[:, sl], nt, preferred_element_type=F32)
        sp_ref[h] = lax.dot_general(q, kp_ref[:, sl], nt, preferred_element_type=F32)
    lse_tile = jnp.zeros((ATT_BLK, V7X_LANES), F32)
    inv_den = []
    for h in range(HEADS_PER_GROUP):
        s_c = jnp.where(mask_c, sc_ref[h] * scale, neg)
        s_p = jnp.where(mask_p, sp_ref[h] * scale, neg)
        mx = jnp.max(jnp.maximum(s_c, s_p), axis=-1, keepdims=True)
        p_c = jnp.exp(s_c - mx)
        p_p = jnp.exp(s_p - mx)
        den = jnp.sum(p_c + p_p, axis=-1, keepdims=True)
        pc_ref[h] = p_c.astype(BF16)
        pp_ref[h] = p_p.astype(BF16)
        inv_den.append(1.0 / den)
        lse_tile = jnp.where(kj == h, mx + jnp.log(den), lse_tile)
    lse_ref[...] = lse_tile
    for h, sl in enumerate(heads):
        o = (_dot(pc_ref[h], vc_ref[:, sl]) + _dot(pp_ref[h], vp_ref[:, sl])) * inv_den[h]
        o_ref[:, sl] = o.astype(BF16)


def _attention_group(proj3, g, dil):
    b, s, _ = proj3.shape
    l = s // dil
    assert l % ATT_BLK == 0
    nb = l // ATT_BLK
    if dil == 1:
        pv, cb, base = proj3, IN_COLS // GROUP_W, 3 * g
    else:
        pv = proj3[:, :, 3 * g * GROUP_W:3 * (g + 1) * GROUP_W].reshape(b, l, dil * 3 * GROUP_W)
        cb, base = 3, 0

    def cur(which):
        return pl.BlockSpec((None, ATT_BLK, GROUP_W), lambda bi, r, c: (bi, c, r * cb + base + which))

    def prev(which):
        return pl.BlockSpec((None, ATT_BLK, GROUP_W),
                            lambda bi, r, c: (bi, jnp.maximum(c - 1, 0), r * cb + base + which))

    o, lse = pl.pallas_call(
        _attn_kernel,
        out_shape=(jax.ShapeDtypeStruct((b, l, dil * GROUP_W), BF16),
                   jax.ShapeDtypeStruct((b, l, dil * V7X_LANES), F32)),
        grid=(b, dil, nb),
        in_specs=[cur(0), prev(1), cur(1), prev(2), cur(2)],
        out_specs=(pl.BlockSpec((None, ATT_BLK, GROUP_W), lambda bi, r, c: (bi, c, r)),
                   pl.BlockSpec((None, ATT_BLK, V7X_LANES), lambda bi, r, c: (bi, c, r))),
        scratch_shapes=[pltpu.VMEM((HEADS_PER_GROUP, ATT_BLK, ATT_BLK), F32)] * 2
        + [pltpu.VMEM((HEADS_PER_GROUP, ATT_BLK, ATT_BLK), BF16)] * 2,
        compiler_params=_params(3, 32),
        name=f"attn_g{g}",
    )(pv, pv, pv, pv, pv)
    return o.reshape(b * s, GROUP_W), lse.reshape(b * s, V7X_LANES)


CMB_TM = 512


def _combine_kernel(o0_ref, o1_ref, o2_ref, l0_ref, l1_ref, l2_ref, att_ref):
    l0, l1, l2 = l0_ref[...], l1_ref[...], l2_ref[...]
    m = jnp.maximum(jnp.maximum(l0, l1), l2)
    e0, e1, e2 = jnp.exp(l0 - m), jnp.exp(l1 - m), jnp.exp(l2 - m)
    den = e0 + e1 + e2
    for g, (o_ref, e) in enumerate(((o0_ref, e0), (o1_ref, e1), (o2_ref, e2))):
        wgt = e / den
        for h in range(HEADS_PER_GROUP):
            sl = slice(h * HEAD_DIM, (h + 1) * HEAD_DIM)
            val = o_ref[:, sl].astype(F32) * wgt[:, h:h + 1]
            att_ref[:, g * GROUP_W + h * HEAD_DIM:g * GROUP_W + (h + 1) * HEAD_DIM] = val.astype(BF16)


def _combine(os, lses):
    t = os[0].shape[0]
    o_spec = pl.BlockSpec((CMB_TM, GROUP_W), lambda i: (i, 0))
    l_spec = pl.BlockSpec((CMB_TM, V7X_LANES), lambda i: (i, 0))
    return pl.pallas_call(
        _combine_kernel,
        out_shape=jax.ShapeDtypeStruct((t, ATT_W), BF16),
        grid=(t // CMB_TM,),
        in_specs=[o_spec] * 3 + [l_spec] * 3,
        out_specs=pl.BlockSpec((CMB_TM, ATT_W), lambda i: (i, 0)),
        compiler_params=_params(1, 32),
        name="attn_combine",
    )(*os, *lses)


LRU_TT, LRU_CW, LRU_ROWS = 512, 512, 32
LRU_CHUNK = LRU_TT // V7X_SUBLANES
LRU_PAD = (LRU_CONV - 1) * V7X_SUBLANES


def _compose_over_sublanes(a, b, row):
    for s in (1, 2, 4):
        m = row >= s
        b = jnp.where(m, a * pltpu.roll(b, s, axis=0) + b, b)
        a = jnp.where(m, a * pltpu.roll(a, s, axis=0), a)
    return a, b


def _lru_kernel(xr_ref, gy_ref, cw_ref, cb_ref, wr_ref, br_ref, wi_ref, bi_ref, lam_ref,
                o_ref, stage_ref, ext_ref, halo_ref, carry_ref, xc_ref, xcb_ref, a_ref, b_ref):
    tt, sub, pad = LRU_TT, V7X_SUBLANES, LRU_PAD
    lane_blocks = [slice(c * V7X_LANES, (c + 1) * V7X_LANES) for c in range(LRU_CW // V7X_LANES)]
    row = lax.broadcasted_iota(jnp.int32, (sub, LRU_CW), 0)

    @pl.when(pl.program_id(2) == 0)
    def _():
        halo_ref[...] = jnp.zeros(halo_ref.shape, F32)
        carry_ref[...] = jnp.zeros(carry_ref.shape, F32)

    for c, cols in enumerate(lane_blocks):
        stage_ref[c] = xr_ref[:, cols].astype(F32)
    for k in range(LRU_CHUNK):
        for c, cols in enumerate(lane_blocks):
            ext_ref[pad + k * sub:pad + (k + 1) * sub, cols] = stage_ref[c, pl.ds(k, sub, stride=LRU_CHUNK), :]
    for m in range(LRU_CONV - 1):
        cur = ext_ref[tt + m * sub:tt + (m + 1) * sub, :]
        old = halo_ref[m * sub:(m + 1) * sub, :]
        ext_ref[m * sub:(m + 1) * sub, :] = jnp.where(
            row == 0, pltpu.roll(old, 1, axis=0), pltpu.roll(cur, 1, axis=0))
    halo_ref[...] = ext_ref[tt:tt + pad, :]

    cw = cw_ref[...]
    cb = cb_ref[...]
    for r0 in range(0, tt, LRU_ROWS):
        xc = ext_ref[r0:r0 + LRU_ROWS, :] * cw[0:1]
        for m in range(1, LRU_CONV):
            xc = xc + ext_ref[r0 + m * sub:r0 + m * sub + LRU_ROWS, :] * cw[m:m + 1]
        xc = xc + cb
        xc_ref[r0:r0 + LRU_ROWS, :] = xc
        xcb_ref[r0:r0 + LRU_ROWS, :] = xc.astype(BF16)

    for n in range(LRU_CW // LRU_BW):
        cols = slice(n * LRU_BW, (n + 1) * LRU_BW)
        a_ref[:, cols] = _dot(xcb_ref[:, cols], wr_ref[n])
        b_ref[:, cols] = _dot(xcb_ref[:, cols], wi_ref[n])

    lam = lam_ref[...]
    neg_c_softplus = -LRU_C * (jnp.maximum(-lam, 0.0) + jnp.log1p(jnp.exp(-jnp.abs(lam))))
    br, bi = br_ref[...], bi_ref[...]
    for r0 in range(0, tt, LRU_ROWS):
        rows = slice(r0, r0 + LRU_ROWS)
        r = _sigmoid(a_ref[rows, :] + br)
        i = _sigmoid(b_ref[rows, :] + bi)
        log_a = r * neg_c_softplus
        a = jnp.exp(log_a)
        a_ref[rows, :] = a
        b_ref[rows, :] = jnp.sqrt(-jnp.tanh(log_a) * (a * a + 1.0)) * i * xc_ref[rows, :]

    h_end = b_ref[0:sub, :]
    decay = a_ref[0:sub, :]
    for k in range(1, LRU_CHUNK):
        av = a_ref[k * sub:(k + 1) * sub, :]
        h_end = av * h_end + b_ref[k * sub:(k + 1) * sub, :]
        decay = av * decay
    pa, pb = _compose_over_sublanes(decay, h_end, row)
    h_in = carry_ref[...]
    h_out = pa * h_in + pb
    carry_ref[...] = jnp.broadcast_to(h_out[sub - 1:sub, :], (sub, LRU_CW))
    h = jnp.where(row == 0, h_in, pltpu.roll(h_out, 1, axis=0))
    for k in range(LRU_CHUNK):
        h = a_ref[k * sub:(k + 1) * sub, :] * h + b_ref[k * sub:(k + 1) * sub, :]
        for c, cols in enumerate(lane_blocks):
            stage_ref[c, pl.ds(k, sub, stride=LRU_CHUNK), :] = h[:, cols]
    for c, cols in enumerate(lane_blocks):
        o_ref[:, cols] = (stage_ref[c] * gy_ref[:, cols].astype(F32)).astype(BF16)


def _rglru(proj3, cw, cb, wr_bf, br, wi_bf, bi, lam):
    b, s, _ = proj3.shape
    assert OFF_XR % LRU_CW == 0 and OFF_YR % LRU_CW == 0 and s % LRU_TT == 0
    xr_blk, gy_blk = OFF_XR // LRU_CW, OFF_YR // LRU_CW
    gate_blocks = LRU_CW // LRU_BW
    vec = lambda: pl.BlockSpec((1, LRU_CW), lambda n, bi_, t: (0, n))
    gate_w = lambda: pl.BlockSpec((gate_blocks, LRU_BW, LRU_BW), lambda n, bi_, t: (n, 0, 0))
    out = pl.pallas_call(
        _lru_kernel,
        out_shape=jax.ShapeDtypeStruct((b, s, LRU_W), BF16),
        grid=(LRU_W // LRU_CW, b, s // LRU_TT),
        in_specs=[
            pl.BlockSpec((None, LRU_TT, LRU_CW), lambda n, bi_, t: (bi_, t, xr_blk + n)),
            pl.BlockSpec((None, LRU_TT, LRU_CW), lambda n, bi_, t: (bi_, t, gy_blk + n)),
            pl.BlockSpec((LRU_CONV, LRU_CW), lambda n, bi_, t: (0, n)),
            vec(),
            gate_w(),
            vec(),
            gate_w(),
            vec(),
            vec(),
        ],
        out_specs=pl.BlockSpec((None, LRU_TT, LRU_CW), lambda n, bi_, t: (bi_, t, n)),
        scratch_shapes=[
            pltpu.VMEM((LRU_CW // V7X_LANES, LRU_TT, V7X_LANES), F32),
            pltpu.VMEM((LRU_TT + LRU_PAD, LRU_CW), F32),
            pltpu.VMEM((LRU_PAD, LRU_CW), F32),
            pltpu.VMEM((V7X_SUBLANES, LRU_CW), F32),
            pltpu.VMEM((LRU_TT, LRU_CW), F32),
            pltpu.VMEM((LRU_TT, LRU_CW), BF16),
            pltpu.VMEM((LRU_TT, LRU_CW), F32),
            pltpu.VMEM((LRU_TT, LRU_CW), F32),
        ],
        compiler_params=_params(3, 32),
        name="rglru",
    )(proj3, proj3, cw, cb, wr_bf, br, wi_bf, bi, lam)
    return out.reshape(b * s, LRU_W)


MRG_TM, MRG_TN = 1024, 512


def _merge_kernel(att_ref, lru_ref, wa_ref, wl_ref, sa_ref, sr_ref, o_ref):
    ya = _dot(att_ref[...], wa_ref[...])
    yl = _dot(lru_ref[...], wl_ref[...])
    o_ref[...] = (sa_ref[...].astype(F32) * ya + sr_ref[...].astype(F32) * yl).astype(BF16)


def _merge(att, lru, wa_bf, wl_bf, proj):
    t = att.shape[0]
    ga_blk, gr_blk = OFF_GA // MRG_TN, OFF_GR // MRG_TN
    return pl.pallas_call(
        _merge_kernel,
        out_shape=jax.ShapeDtypeStruct((t, D_MODEL), BF16),
        grid=(t // MRG_TM, D_MODEL // MRG_TN),
        in_specs=[
            pl.BlockSpec((MRG_TM, ATT_W), lambda i, j: (i, 0)),
            pl.BlockSpec((MRG_TM, LRU_W), lambda i, j: (i, 0)),
            pl.BlockSpec((ATT_W, MRG_TN), lambda i, j: (0, j)),
            pl.BlockSpec((LRU_W, MRG_TN), lambda i, j: (0, j)),
            pl.BlockSpec((MRG_TM, MRG_TN), lambda i, j: (i, ga_blk + j)),
            pl.BlockSpec((MRG_TM, MRG_TN), lambda i, j: (i, gr_blk + j)),
        ],
        out_specs=pl.BlockSpec((MRG_TM, MRG_TN), lambda i, j: (i, j)),
        compiler_params=_params(2, 52),
        name="merge",
    )(att, lru, wa_bf, wl_bf, proj, proj)


RES_TM, RES_TN = 1024, 512


def _res_matmul_kernel(a_ref, w_ref, x_ref, o_ref):
    o_ref[...] = x_ref[...] + _dot(a_ref[...], w_ref[...])


def _res_matmul(a, w_bf, x2, name):
    t, k = a.shape
    n = w_bf.shape[1]
    return pl.pallas_call(
        _res_matmul_kernel,
        out_shape=jax.ShapeDtypeStruct((t, n), F32),
        grid=(t // RES_TM, n // RES_TN),
        in_specs=[
            pl.BlockSpec((RES_TM, k), lambda i, j: (i, 0)),
            pl.BlockSpec((k, RES_TN), lambda i, j: (0, j)),
            pl.BlockSpec((RES_TM, RES_TN), lambda i, j: (i, j)),
        ],
        out_specs=pl.BlockSpec((RES_TM, RES_TN), lambda i, j: (i, j)),
        compiler_params=_params(2, 52),
        name=name,
    )(a, w_bf, x2)


UP_TM, UP_TN, UP_SUB, UP_ROWS = 512, 512, 256, 32
UP_CHUNK = UP_TM // V7X_SUBLANES


def _ffn_up_kernel(seq_tiles, x_ref, g_ref, wg_ref, wv_ref, cwg_ref, cwv_ref, cbg_ref, cbv_ref,
                   o_ref, h_ref, stage_ref, ext_g_ref, ext_v_ref, halo_g_ref, halo_v_ref):
    i, j = pl.program_id(0), pl.program_id(1)
    tm = UP_TM
    sub = V7X_SUBLANES
    pad = (FFN_CONV - 1) * sub

    @pl.when(j == 0)
    def _():
        g = g_ref[...]
        for m in range(tm // sub):
            y = _rmsnorm(x_ref[m * sub:(m + 1) * sub, :], g)
            chunk, k0 = divmod(m * sub, UP_CHUNK)
            for cb in range(D_MODEL // V7X_LANES):
                stage_ref[cb, pl.ds(sub * k0 + chunk, sub, stride=sub), :] = (
                    y[:, cb * V7X_LANES:(cb + 1) * V7X_LANES])
        for cb in range(D_MODEL // V7X_LANES):
            h_ref[:, cb * V7X_LANES:(cb + 1) * V7X_LANES] = stage_ref[cb].astype(BF16)

    @pl.when((i == 0) & (j == 0))
    def _():
        halo_g_ref[...] = jnp.zeros(halo_g_ref.shape, F32)
        halo_v_ref[...] = jnp.zeros(halo_v_ref.shape, F32)

    seq_start = i % seq_tiles == 0
    h = h_ref[...]
    first_row = lax.broadcasted_iota(jnp.int32, (sub, UP_SUB), 0) == 0

    def raw(w_ref, ext_ref, halo_ref, sl):
        ext_ref[pad:pad + tm, sl] = _dot(h, w_ref[:, sl])
        for m in range(FFN_CONV - 1):
            cur = ext_ref[tm + m * sub:tm + (m + 1) * sub, sl]
            old = jnp.where(seq_start, 0.0, halo_ref[j, m * sub:(m + 1) * sub, sl])
            ext_ref[m * sub:(m + 1) * sub, sl] = jnp.where(
                first_row, pltpu.roll(old, 1, axis=0), pltpu.roll(cur, 1, axis=0))
        halo_ref[j, :, sl] = ext_ref[tm:tm + pad, sl]

    def conv(ext_ref, cw, cb, r0, sl):
        y = ext_ref[r0:r0 + UP_ROWS, sl] * cw[0:1]
        for m in range(1, FFN_CONV):
            y = y + ext_ref[r0 + m * sub:r0 + m * sub + UP_ROWS, sl] * cw[m:m + 1]
        return y + cb

    for s in range(UP_TN // UP_SUB):
        sl = slice(s * UP_SUB, (s + 1) * UP_SUB)
        raw(wg_ref, ext_g_ref, halo_g_ref, sl)
        raw(wv_ref, ext_v_ref, halo_v_ref, sl)
        cwg, cwv, cbg, cbv = cwg_ref[:, sl], cwv_ref[:, sl], cbg_ref[:, sl], cbv_ref[:, sl]
        for r0 in range(0, tm, UP_ROWS):
            gate = _gelu_tanh(conv(ext_g_ref, cwg, cbg, r0, sl))
            val = conv(ext_v_ref, cwv, cbv, r0, sl)
            o_ref[r0:r0 + UP_ROWS, sl] = (gate * val).astype(BF16)


def _ffn_up(x2, g, w_bf, cw, cb, seq):
    t = x2.shape[0]
    nj = FFN_DIM // UP_TN
    pad = (FFN_CONV - 1) * V7X_SUBLANES
    assert seq % UP_TM == 0
    return pl.pallas_call(
        functools.partial(_ffn_up_kernel, seq // UP_TM),
        out_shape=jax.ShapeDtypeStruct((t, FFN_DIM), BF16),
        grid=(t // UP_TM, nj),
        in_specs=[
            pl.BlockSpec((UP_TM, D_MODEL), lambda i, j: (i, 0)),
            pl.BlockSpec((1, D_MODEL), lambda i, j: (0, 0)),
            pl.BlockSpec((D_MODEL, UP_TN), lambda i, j: (0, j)),
            pl.BlockSpec((D_MODEL, UP_TN), lambda i, j: (0, nj + j)),
            pl.BlockSpec((FFN_CONV, UP_TN), lambda i, j: (0, j)),
            pl.BlockSpec((FFN_CONV, UP_TN), lambda i, j: (0, nj + j)),
            pl.BlockSpec((1, UP_TN), lambda i, j: (0, j)),
            pl.BlockSpec((1, UP_TN), lambda i, j: (0, nj + j)),
        ],
        out_specs=pl.BlockSpec((UP_TM, UP_TN), lambda i, j: (i, j)),
        scratch_shapes=[
            pltpu.VMEM((UP_TM, D_MODEL), BF16),
            pltpu.VMEM((D_MODEL // V7X_LANES, UP_TM, V7X_LANES), F32),
            pltpu.VMEM((UP_TM + pad, UP_TN), F32),
            pltpu.VMEM((UP_TM + pad, UP_TN), F32),
            pltpu.VMEM((nj, pad, UP_TN), F32),
            pltpu.VMEM((nj, pad, UP_TN), F32),
        ],
        compiler_params=_params(2, 56),
        name="ffn_up",
    )(x2, g, w_bf, w_bf, cw, cw, cb, cb)


DN_TM, DN_TN, DN_TK = 1024, 1024, 2048


def _ffn_down_kernel(a_ref, w_ref, x_ref, o_ref, acc_ref):
    k = pl.program_id(2)
    sub = V7X_SUBLANES
    lane_blocks = [slice(c * V7X_LANES, (c + 1) * V7X_LANES) for c in range(DN_TN // V7X_LANES)]

    @pl.when(k == 0)
    def _():
        d = _dot(a_ref[...], w_ref[...])
        for c, cols in enumerate(lane_blocks):
            acc_ref[c] = d[:, cols]

    @pl.when(k > 0)
    def _():
        d = _dot(a_ref[...], w_ref[...])
        for c, cols in enumerate(lane_blocks):
            acc_ref[c] += d[:, cols]

    @pl.when(k == pl.num_programs(2) - 1)
    def _():
        for blk in range(0, DN_TM, UP_TM):
            for m in range(UP_TM // sub):
                chunk, k0 = divmod(m * sub, UP_CHUNK)
                rows = slice(blk + m * sub, blk + (m + 1) * sub)
                src = pl.ds(blk + sub * k0 + chunk, sub, stride=sub)
                for c, cols in enumerate(lane_blocks):
                    o_ref[rows, cols] = x_ref[rows, cols] + acc_ref[c, src, :]


def _ffn_down(a, w_bf, x2):
    t, kk = a.shape
    n = w_bf.shape[1]
    return pl.pallas_call(
        _ffn_down_kernel,
        out_shape=jax.ShapeDtypeStruct((t, n), F32),
        grid=(t // DN_TM, n // DN_TN, kk // DN_TK),
        in_specs=[
            pl.BlockSpec((DN_TM, DN_TK), lambda i, j, k: (i, k)),
            pl.BlockSpec((DN_TK, DN_TN), lambda i, j, k: (k, j)),
            pl.BlockSpec((DN_TM, DN_TN), lambda i, j, k: (i, j)),
        ],
        out_specs=pl.BlockSpec((DN_TM, DN_TN), lambda i, j, k: (i, j)),
        scratch_shapes=[pltpu.VMEM((DN_TN // V7X_LANES, DN_TM, V7X_LANES), F32)],
        compiler_params=_params(3, 48),
        name="ffn_down",
    )(a, w_bf, x2)


PLE_TM, PLE_TN = 512, 512


def _ple_kernel(x_ref, p_ref, gg_ref, wg_ref, wp_ref, ge_ref, gf_ref, o_ref, h_ref):
    j = pl.program_id(1)

    @pl.when(j == 0)
    def _():
        h_ref[...] = _rmsnorm(x_ref[...], gg_ref[...]).astype(BF16)
        for c in range(0, D_MODEL, PLE_TN):
            o_ref[:, c:c + PLE_TN] = _dot(p_ref[...], wp_ref[:, c:c + PLE_TN])
        o_ref[...] = _rmsnorm(o_ref[...], ge_ref[...])

    col = pl.multiple_of(j * PLE_TN, PLE_TN)
    gate = _sigmoid(_dot(h_ref[...], wg_ref[...]))
    o_ref[:, pl.ds(col, PLE_TN)] = x_ref[:, pl.ds(col, PLE_TN)] + gate * o_ref[:, pl.ds(col, PLE_TN)]

    @pl.when(j == pl.num_programs(1) - 1)
    def _():
        o_ref[...] = _rmsnorm(o_ref[...], gf_ref[...])


def _ple(x2, p_bf, gg, wg_bf, wp_bf, ge, gf):
    t = x2.shape[0]
    row = lambda: pl.BlockSpec((1, D_MODEL), lambda i, j: (0, 0))
    return pl.pallas_call(
        _ple_kernel,
        out_shape=jax.ShapeDtypeStruct((t, D_MODEL), F32),
        grid=(t // PLE_TM, D_MODEL // PLE_TN),
        in_specs=[
            pl.BlockSpec((PLE_TM, D_MODEL), lambda i, j: (i, 0)),
            pl.BlockSpec((PLE_TM, PLE_DIM), lambda i, j: (i, 0)),
            row(),
            pl.BlockSpec((D_MODEL, PLE_TN), lambda i, j: (0, j)),
            pl.BlockSpec((PLE_DIM, D_MODEL), lambda i, j: (0, 0)),
            row(),
            row(),
        ],
        out_specs=pl.BlockSpec((PLE_TM, D_MODEL), lambda i, j: (i, 0)),
        scratch_shapes=[pltpu.VMEM((PLE_TM, D_MODEL), BF16)],
        compiler_params=_params(2, 56),
        name="ple_final",
    )(x2, p_bf, gg, wg_bf, wp_bf, ge, gf)


def _group_major_qkv(w):
    cols = [w[:, base + g * GROUP_W:base + (g + 1) * GROUP_W]
            for g in range(N_GROUPS) for base in (0, ATT_W, 2 * ATT_W)]
    return jnp.concatenate(cols + [w[:, OFF_XR:]], axis=1)


def kernel(x, p, mix_norm_g, w_in, lru_conv_w, lru_conv_b, w_rgate, b_rgate, w_igate, b_igate, lru_lambda, w_att_branch, w_lru_branch, w_out, ffn_norm_g, w_up, ffn_conv_w, ffn_conv_b, w_down, ple_proj, ple_norm_g, ple_gate_norm_g, ple_gate_w, final_norm_g):
    b, s, d = x.shape
    depth = w_in.shape[0]
    t = b * s
    row = lambda v: v.reshape(1, -1)
    xf = x.reshape(t, d)
    for l in range(depth):
        proj = _in_proj(xf, row(mix_norm_g[l]), _group_major_qkv(w_in[l]).astype(BF16))
        proj3 = proj.reshape(b, s, IN_COLS)
        os, lses = zip(*[_attention_group(proj3, g, dil) for g, (_, dil) in enumerate(PATTERNS)])
        att = _combine(os, lses)
        lru = _rglru(proj3, lru_conv_w[l], row(lru_conv_b[l]), w_rgate[l].astype(BF16), row(b_rgate[l]),
                     w_igate[l].astype(BF16), row(b_igate[l]), row(lru_lambda[l]))
        merged = _merge(att, lru, w_att_branch[l].astype(BF16), w_lru_branch[l].astype(BF16), proj)
        xf = _res_matmul(merged, w_out[l].astype(BF16), xf, "out_proj")
        gact = _ffn_up(xf, row(ffn_norm_g[l]), w_up[l].astype(BF16), ffn_conv_w[l], row(ffn_conv_b[l]), s)
        xf = _ffn_down(gact, w_down[l].astype(BF16), xf)
        assert l == depth - 1 == 0
        xf = _ple(xf, p[l].reshape(t, PLE_DIM).astype(BF16), row(ple_gate_norm_g[l]),
                  ple_gate_w[l].astype(BF16), ple_proj[l].astype(BF16), row(ple_norm_g[l]), row(final_norm_g))
    return xf.reshape(b, s, d)
```

```python
import functools
import math

import jax
import jax.numpy as jnp
from jax import lax
from jax.experimental import pallas as pl
from jax.experimental.pallas import tpu as pltpu

F32 = jnp.float32
BF16 = jnp.bfloat16

D_MODEL = 4096
HEAD_DIM = 128
HEADS_PER_GROUP = 8
PATTERNS = ((128, 1), (512, 4), (2048, 16))
N_GROUPS = len(PATTERNS)
GROUP_W = HEADS_PER_GROUP * HEAD_DIM
ATT_W = N_GROUPS * GROUP_W
ATT_BLK = 128
LRU_W = D_MODEL
LRU_BLOCKS = 16
LRU_BW = LRU_W // LRU_BLOCKS
LRU_C = 8.0
LRU_CONV = 4
FFN_DIM = 3 * D_MODEL
FFN_CONV = 3
PLE_DIM = 256
EPS = 1e-6
OFF_Q, OFF_K, OFF_V = 0, ATT_W, 2 * ATT_W
OFF_XR = 3 * ATT_W
OFF_YR = OFF_XR + LRU_W
OFF_GA = OFF_YR + LRU_W
OFF_GR = OFF_GA + D_MODEL
IN_COLS = OFF_GR + D_MODEL

V7X_VMEM_BYTES = 64 * 1024 * 1024
V7X_SUBLANES = 8
V7X_LANES = 128
MIB = 1024 * 1024


def _params(n_axes, vmem_mib):
    assert vmem_mib * MIB < V7X_VMEM_BYTES
    return pltpu.CompilerParams(
        dimension_semantics=("arbitrary",) * n_axes,
        vmem_limit_bytes=vmem_mib * MIB,
    )


def _sigmoid(v):
    return 1.0 / (1.0 + jnp.exp(-v))


def _gelu_tanh(v):
    c = math.sqrt(2.0 / math.pi)
    return v * (0.5 * (1.0 + jnp.tanh(c * (v + 0.044715 * (v * v * v)))))


def _rmsnorm(v, g):
    return v * lax.rsqrt(jnp.mean(v * v, axis=-1, keepdims=True) + EPS) * g


def _rmsnorm_rows(src_ref, g, dst_ref):
    rows = 2 * V7X_SUBLANES if dst_ref.dtype == BF16 else V7X_SUBLANES
    for r0 in range(0, src_ref.shape[0], rows):
        dst_ref[r0:r0 + rows, :] = _rmsnorm(src_ref[r0:r0 + rows, :], g).astype(dst_ref.dtype)


def _dot(a, b):
    return jnp.dot(a, b, preferred_element_type=F32)


IN_TM, IN_TN, IN_SUB = 512, 1024, 256


def _in_proj_kernel(x_ref, g_ref, w_ref, o_ref, h_ref):
    j = pl.program_id(1)

    @pl.when(j == 0)
    def _():
        _rmsnorm_rows(x_ref, g_ref[...], h_ref)

    def body(act):
        h = h_ref[...]
        for s in range(IN_TN // IN_SUB):
            sl = slice(s * IN_SUB, (s + 1) * IN_SUB)
            o_ref[:, sl] = act(_dot(h, w_ref[:, sl])).astype(BF16)

    j_yr, j_ga = OFF_YR // IN_TN, OFF_GA // IN_TN
    pl.when(j < j_yr)(functools.partial(body, lambda v: v))
    pl.when((j >= j_yr) & (j < j_ga))(functools.partial(body, _gelu_tanh))
    pl.when(j >= j_ga)(functools.partial(body, _sigmoid))


def _in_proj(x2, g, w_bf):
    t = x2.shape[0]
    assert OFF_YR % IN_TN == 0 and OFF_GA % IN_TN == 0
    return pl.pallas_call(
        _in_proj_kernel,
        out_shape=jax.ShapeDtypeStruct((t, IN_COLS), BF16),
        grid=(t // IN_TM, IN_COLS // IN_TN),
        in_specs=[
            pl.BlockSpec((IN_TM, D_MODEL), lambda i, j: (i, 0)),
            pl.BlockSpec((1, D_MODEL), lambda i, j: (0, 0)),
            pl.BlockSpec((D_MODEL, IN_TN), lambda i, j: (0, j)),
        ],
        out_specs=pl.BlockSpec((IN_TM, IN_TN), lambda i, j: (i, j)),
        scratch_shapes=[pltpu.VMEM((IN_TM, D_MODEL), BF16)],
        compiler_params=_params(2, 56),
        name="in_proj",
    )(x2, g, w_bf)


def _attend_heads(q, kc, kp, vc, vp, n_heads, has_prev, sc_ref, sp_ref, pc_ref, pp_ref):
    qi = lax.broadcasted_iota(jnp.int32, (ATT_BLK, ATT_BLK), 0)
    kj = lax.broadcasted_iota(jnp.int32, (ATT_BLK, ATT_BLK), 1)
    mask_c = kj <= qi
    mask_p = (kj >= qi) & has_prev
    neg = jnp.finfo(F32).min
    scale = HEAD_DIM ** -0.5
    nt = (((1,), (1,)), ((), ()))
    for h in range(n_heads):
        sc_ref[h] = lax.dot_general(q(h), kc(h), nt, preferred_element_type=F32)
        sp_ref[h] = lax.dot_general(q(h), kp(h), nt, preferred_element_type=F32)
    inv_den, lses = [], []
    for h in range(n_heads):
        s_c = jnp.where(mask_c, sc_ref[h] * scale, neg)
        s_p = jnp.where(mask_p, sp_ref[h] * scale, neg)
        mx = jnp.max(jnp.maximum(s_c, s_p), axis=-1, keepdims=True)
        p_c = jnp.exp(s_c - mx)
        p_p = jnp.exp(s_p - mx)
        den = jnp.sum(p_c + p_p, axis=-1, keepdims=True)
        pc_ref[h] = p_c.astype(BF16)
        pp_ref[h] = p_p.astype(BF16)
        inv_den.append(1.0 / den)
        lses.append(mx + jnp.log(den))
    outs = [(_dot(pc_ref[h], vc(h)) + _dot(pp_ref[h], vp(h))) * inv_den[h] for h in range(n_heads)]
    return outs, lses


def _lse_tile(lses, first_lane):
    lane = lax.broadcasted_iota(jnp.int32, (ATT_BLK, V7X_LANES), 1)
    tile = jnp.zeros((ATT_BLK, V7X_LANES), F32)
    for h, lse in enumerate(lses):
        tile = jnp.where(lane == first_lane + h, lse, tile)
    return tile


def _head_cols(h):
    return slice(h * HEAD_DIM, (h + 1) * HEAD_DIM)


def _attn_kernel(q_ref, kp_ref, kc_ref, vp_ref, vc_ref, o_ref, lse_ref, sc_ref, sp_ref, pc_ref, pp_ref):
    load = lambda ref: (lambda h: ref[:, _head_cols(h)])
    outs, lses = _attend_heads(load(q_ref), load(kc_ref), load(kp_ref), load(vc_ref), load(vp_ref),
                               HEADS_PER_GROUP, pl.program_id(1) > 0, sc_ref, sp_ref, pc_ref, pp_ref)
    lse_ref[...] = _lse_tile(lses, 0)
    for h, o in enumerate(outs):
        o_ref[:, _head_cols(h)] = o.astype(BF16)


def _score_scratch(n_heads):
    return ([pltpu.VMEM((n_heads, ATT_BLK, ATT_BLK), F32)] * 2
            + [pltpu.VMEM((n_heads, ATT_BLK, ATT_BLK), BF16)] * 2)


def _attention_dense(proj3, g):
    b, s, _ = proj3.shape
    assert s % ATT_BLK == 0

    def cur(which):
        return pl.BlockSpec((None, ATT_BLK, GROUP_W), lambda bi, c: (bi, c, 3 * g + which))

    def prev(which):
        return pl.BlockSpec((None, ATT_BLK, GROUP_W), lambda bi, c: (bi, jnp.maximum(c - 1, 0), 3 * g + which))

    o, lse = pl.pallas_call(
        _attn_kernel,
        out_shape=(jax.ShapeDtypeStruct((b, s, GROUP_W), BF16),
                   jax.ShapeDtypeStruct((b, s, V7X_LANES), F32)),
        grid=(b, s // ATT_BLK),
        in_specs=[cur(0), prev(1), cur(1), prev(2), cur(2)],
        out_specs=(pl.BlockSpec((None, ATT_BLK, GROUP_W), lambda bi, c: (bi, c, 0)),
                   pl.BlockSpec((None, ATT_BLK, V7X_LANES), lambda bi, c: (bi, c, 0))),
        scratch_shapes=_score_scratch(HEADS_PER_GROUP),
        compiler_params=_params(2, 32),
        name=f"attn_g{g}",
    )(proj3, proj3, proj3, proj3, proj3)
    return o.reshape(b * s, GROUP_W), lse.reshape(b * s, V7X_LANES)


ATT_HW = 512
ATT_NH = ATT_HW // HEAD_DIM
ATT_RES = 2
ATT_PERM = 256


def _attn_strided_kernel(dil, q_ref, k_ref, v_ref, o_ref, lse_ref, qd_ref, kd_ref, vd_ref, od_ref, ls_ref,
                         sc_ref, sp_ref, pc_ref, pp_ref):
    c, half = pl.program_id(1), pl.program_id(2)
    slot = c % 2
    kc_ref, kp_ref = kd_ref.at[half, slot], kd_ref.at[half, 1 - slot]
    vc_ref, vp_ref = vd_ref.at[half, slot], vd_ref.at[half, 1 - slot]
    ch = ATT_BLK * dil
    n = ATT_PERM // dil
    row = lax.broadcasted_iota(jnp.int32, (ATT_PERM, ATT_PERM), 0)
    col = lax.broadcasted_iota(jnp.int32, (ATT_PERM, ATT_PERM), 1)
    to_residues = jnp.where(col == (row % n) * dil + row // n, 1.0, 0.0).astype(BF16)
    to_tokens = jnp.where(col == (row % dil) * n + row // dil, 1.0, 0.0).astype(BF16)

    @pl.when(c == 0)
    def _():
        kp_ref[...] = jnp.zeros(kp_ref.shape, BF16)
        vp_ref[...] = jnp.zeros(vp_ref.shape, BF16)

    def split(src_ref, dst_ref):
        for m in range(ch // ATT_PERM):
            y = _dot(to_residues, src_ref[m * ATT_PERM:(m + 1) * ATT_PERM, :]).astype(BF16)
            for r in range(dil):
                dst_ref[r, m * n:(m + 1) * n, :] = y[r * n:(r + 1) * n, :]

    split(q_ref, qd_ref)
    split(k_ref, kc_ref)
    split(v_ref, vc_ref)

    for r0 in range(0, dil, ATT_RES):
        load = lambda ref: (lambda v: ref[r0 + v // ATT_NH, :, _head_cols(v % ATT_NH)])
        outs, lses = _attend_heads(load(qd_ref), load(kc_ref), load(kp_ref), load(vc_ref), load(vp_ref),
                                   ATT_RES * ATT_NH, c > 0, sc_ref, sp_ref, pc_ref, pp_ref)
        for i in range(ATT_RES):
            ls_ref[pl.ds(r0 + i, ATT_BLK, stride=dil), :] = _lse_tile(
                lses[i * ATT_NH:(i + 1) * ATT_NH], half * ATT_NH)
            for h in range(ATT_NH):
                od_ref[r0 + i, :, _head_cols(h)] = outs[i * ATT_NH + h].astype(BF16)
    for m in range(ch // ATT_PERM):
        regrouped = jnp.concatenate([od_ref[r, m * n:(m + 1) * n, :] for r in range(dil)], axis=0)
        o_ref[m * ATT_PERM:(m + 1) * ATT_PERM, :] = _dot(to_tokens, regrouped).astype(BF16)

    @pl.when(half == 0)
    def _():
        lse_ref[...] = ls_ref[...]

    @pl.when(half > 0)
    def _():
        lse_ref[...] += ls_ref[...]


def _attention_strided(proj3, g, dil):
    b, s, _ = proj3.shape
    ch = ATT_BLK * dil
    halves = GROUP_W // ATT_HW
    assert s % ch == 0
    col = lambda which: (lambda bi, c, hf: (bi, c, (3 * g + which) * halves + hf))
    blk = lambda which: pl.BlockSpec((None, ch, ATT_HW), col(which))
    split_kv = pltpu.VMEM((halves, 2, dil, ATT_BLK, ATT_HW), BF16)
    o, lse = pl.pallas_call(
        functools.partial(_attn_strided_kernel, dil),
        out_shape=(jax.ShapeDtypeStruct((b, s, GROUP_W), BF16),
                   jax.ShapeDtypeStruct((b, s, V7X_LANES), F32)),
        grid=(b, s // ch, halves),
        in_specs=[blk(0), blk(1), blk(2)],
        out_specs=(pl.BlockSpec((None, ch, ATT_HW), lambda bi, c, hf: (bi, c, hf)),
                   pl.BlockSpec((None, ch, V7X_LANES), lambda bi, c, hf: (bi, c, 0))),
        scratch_shapes=[
            pltpu.VMEM((dil, ATT_BLK, ATT_HW), BF16),
            split_kv,
            split_kv,
            pltpu.VMEM((dil, ATT_BLK, ATT_HW), BF16),
            pltpu.VMEM((ch, V7X_LANES), F32),
        ] + _score_scratch(ATT_RES * ATT_NH),
        compiler_params=_params(3, 56),
        name=f"attn_g{g}",
    )(proj3, proj3, proj3)
    return o.reshape(b * s, GROUP_W), lse.reshape(b * s, V7X_LANES)


CMB_TM = 512


def _combine_kernel(o0_ref, o1_ref, o2_ref, l0_ref, l1_ref, l2_ref, att_ref):
    l0, l1, l2 = l0_ref[...], l1_ref[...], l2_ref[...]
    m = jnp.maximum(jnp.maximum(l0, l1), l2)
    e0, e1, e2 = jnp.exp(l0 - m), jnp.exp(l1 - m), jnp.exp(l2 - m)
    den = e0 + e1 + e2
    for g, (o_ref, e) in enumerate(((o0_ref, e0), (o1_ref, e1), (o2_ref, e2))):
        wgt = e / den
        for h in range(HEADS_PER_GROUP):
            sl = slice(h * HEAD_DIM, (h + 1) * HEAD_DIM)
            val = o_ref[:, sl].astype(F32) * wgt[:, h:h + 1]
            att_ref[:, g * GROUP_W + h * HEAD_DIM:g * GROUP_W + (h + 1) * HEAD_DIM] = val.astype(BF16)


def _combine(os, lses):
    t = os[0].shape[0]
    o_spec = pl.BlockSpec((CMB_TM, GROUP_W), lambda i: (i, 0))
    l_spec = pl.BlockSpec((CMB_TM, V7X_LANES), lambda i: (i, 0))
    return pl.pallas_call(
        _combine_kernel,
        out_shape=jax.ShapeDtypeStruct((t, ATT_W), BF16),
        grid=(t // CMB_TM,),
        in_specs=[o_spec] * 3 + [l_spec] * 3,
        out_specs=pl.BlockSpec((CMB_TM, ATT_W), lambda i: (i, 0)),
        compiler_params=_params(1, 32),
        name="attn_combine",
    )(*os, *lses)


LRU_TT, LRU_CW, LRU_ROWS = 512, 512, 32
LRU_CHUNK = LRU_TT // V7X_SUBLANES
LRU_PAD = (LRU_CONV - 1) * V7X_SUBLANES


def _compose_over_sublanes(a, b, row):
    for s in (1, 2, 4):
        m = row >= s
        b = jnp.where(m, a * pltpu.roll(b, s, axis=0) + b, b)
        a = jnp.where(m, a * pltpu.roll(a, s, axis=0), a)
    return a, b


def _lru_kernel(xr_ref, gy_ref, cw_ref, cb_ref, wr_ref, br_ref, wi_ref, bi_ref, lam_ref,
                o_ref, stage_ref, ext_ref, halo_ref, carry_ref, xc_ref, xcb_ref, a_ref, b_ref):
    tt, sub, pad = LRU_TT, V7X_SUBLANES, LRU_PAD
    lane_blocks = [slice(c * V7X_LANES, (c + 1) * V7X_LANES) for c in range(LRU_CW // V7X_LANES)]
    row = lax.broadcasted_iota(jnp.int32, (sub, LRU_CW), 0)

    @pl.when(pl.program_id(2) == 0)
    def _():
        halo_ref[...] = jnp.zeros(halo_ref.shape, F32)
        carry_ref[...] = jnp.zeros(carry_ref.shape, F32)

    for c, cols in enumerate(lane_blocks):
        stage_ref[c] = xr_ref[:, cols].astype(F32)
    for k in range(LRU_CHUNK):
        for c, cols in enumerate(lane_blocks):
            ext_ref[pad + k * sub:pad + (k + 1) * sub, cols] = stage_ref[c, pl.ds(k, sub, stride=LRU_CHUNK), :]
    for m in range(LRU_CONV - 1):
        cur = ext_ref[tt + m * sub:tt + (m + 1) * sub, :]
        old = halo_ref[m * sub:(m + 1) * sub, :]
        ext_ref[m * sub:(m + 1) * sub, :] = jnp.where(
            row == 0, pltpu.roll(old, 1, axis=0), pltpu.roll(cur, 1, axis=0))
    halo_ref[...] = ext_ref[tt:tt + pad, :]

    cw = cw_ref[...]
    cb = cb_ref[...]
    for r0 in range(0, tt, LRU_ROWS):
        xc = ext_ref[r0:r0 + LRU_ROWS, :] * cw[0:1]
        for m in range(1, LRU_CONV):
            xc = xc + ext_ref[r0 + m * sub:r0 + m * sub + LRU_ROWS, :] * cw[m:m + 1]
        xc = xc + cb
        xc_ref[r0:r0 + LRU_ROWS, :] = xc
        xcb_ref[r0:r0 + LRU_ROWS, :] = xc.astype(BF16)

    for n in range(LRU_CW // LRU_BW):
        cols = slice(n * LRU_BW, (n + 1) * LRU_BW)
        a_ref[:, cols] = _dot(xcb_ref[:, cols], wr_ref[n])
        b_ref[:, cols] = _dot(xcb_ref[:, cols], wi_ref[n])

    lam = lam_ref[...]
    neg_c_softplus = -LRU_C * (jnp.maximum(-lam, 0.0) + jnp.log1p(jnp.exp(-jnp.abs(lam))))
    br, bi = br_ref[...], bi_ref[...]
    for r0 in range(0, tt, LRU_ROWS):
        rows = slice(r0, r0 + LRU_ROWS)
        r = _sigmoid(a_ref[rows, :] + br)
        i = _sigmoid(b_ref[rows, :] + bi)
        log_a = r * neg_c_softplus
        a = jnp.exp(log_a)
        a_ref[rows, :] = a
        b_ref[rows, :] = jnp.sqrt(-jnp.tanh(log_a) * (a * a + 1.0)) * i * xc_ref[rows, :]

    h_end = b_ref[0:sub, :]
    decay = a_ref[0:sub, :]
    for k in range(1, LRU_CHUNK):
        av = a_ref[k * sub:(k + 1) * sub, :]
        h_end = av * h_end + b_ref[k * sub:(k + 1) * sub, :]
        decay = av * decay
    pa, pb = _compose_over_sublanes(decay, h_end, row)
    h_in = carry_ref[...]
    h_out = pa * h_in + pb
    carry_ref[...] = jnp.broadcast_to(h_out[sub - 1:sub, :], (sub, LRU_CW))
    h = jnp.where(row == 0, h_in, pltpu.roll(h_out, 1, axis=0))
    for k in range(LRU_CHUNK):
        h = a_ref[k * sub:(k + 1) * sub, :] * h + b_ref[k * sub:(k + 1) * sub, :]
        for c, cols in enumerate(lane_blocks):
            stage_ref[c, pl.ds(k, sub, stride=LRU_CHUNK), :] = h[:, cols]
    for c, cols in enumerate(lane_blocks):
        o_ref[:, cols] = (stage_ref[c] * gy_ref[:, cols].astype(F32)).astype(BF16)


def _rglru(proj3, cw, cb, wr_bf, br, wi_bf, bi, lam):
    b, s, _ = proj3.shape
    assert OFF_XR % LRU_CW == 0 and OFF_YR % LRU_CW == 0 and s % LRU_TT == 0
    xr_blk, gy_blk = OFF_XR // LRU_CW, OFF_YR // LRU_CW
    gate_blocks = LRU_CW // LRU_BW
    vec = lambda: pl.BlockSpec((1, LRU_CW), lambda n, bi_, t: (0, n))
    gate_w = lambda: pl.BlockSpec((gate_blocks, LRU_BW, LRU_BW), lambda n, bi_, t: (n, 0, 0))
    out = pl.pallas_call(
        _lru_kernel,
        out_shape=jax.ShapeDtypeStruct((b, s, LRU_W), BF16),
        grid=(LRU_W // LRU_CW, b, s // LRU_TT),
        in_specs=[
            pl.BlockSpec((None, LRU_TT, LRU_CW), lambda n, bi_, t: (bi_, t, xr_blk + n)),
            pl.BlockSpec((None, LRU_TT, LRU_CW), lambda n, bi_, t: (bi_, t, gy_blk + n)),
            pl.BlockSpec((LRU_CONV, LRU_CW), lambda n, bi_, t: (0, n)),
            vec(),
            gate_w(),
            vec(),
            gate_w(),
            vec(),
            vec(),
        ],
        out_specs=pl.BlockSpec((None, LRU_TT, LRU_CW), lambda n, bi_, t: (bi_, t, n)),
        scratch_shapes=[
            pltpu.VMEM((LRU_CW // V7X_LANES, LRU_TT, V7X_LANES), F32),
            pltpu.VMEM((LRU_TT + LRU_PAD, LRU_CW), F32),
            pltpu.VMEM((LRU_PAD, LRU_CW), F32),
            pltpu.VMEM((V7X_SUBLANES, LRU_CW), F32),
            pltpu.VMEM((LRU_TT, LRU_CW), F32),
            pltpu.VMEM((LRU_TT, LRU_CW), BF16),
            pltpu.VMEM((LRU_TT, LRU_CW), F32),
            pltpu.VMEM((LRU_TT, LRU_CW), F32),
        ],
        compiler_params=_params(3, 32),
        name="rglru",
    )(proj3, proj3, cw, cb, wr_bf, br, wi_bf, bi, lam)
    return out.reshape(b * s, LRU_W)


MRG_TM, MRG_TN = 1024, 512


def _merge_kernel(att_ref, lru_ref, wa_ref, wl_ref, sa_ref, sr_ref, o_ref):
    ya = _dot(att_ref[...], wa_ref[...])
    yl = _dot(lru_ref[...], wl_ref[...])
    o_ref[...] = (sa_ref[...].astype(F32) * ya + sr_ref[...].astype(F32) * yl).astype(BF16)


def _merge(att, lru, wa_bf, wl_bf, proj):
    t = att.shape[0]
    ga_blk, gr_blk = OFF_GA // MRG_TN, OFF_GR // MRG_TN
    return pl.pallas_call(
        _merge_kernel,
        out_shape=jax.ShapeDtypeStruct((t, D_MODEL), BF16),
        grid=(t // MRG_TM, D_MODEL // MRG_TN),
        in_specs=[
            pl.BlockSpec((MRG_TM, ATT_W), lambda i, j: (i, 0)),
            pl.BlockSpec((MRG_TM, LRU_W), lambda i, j: (i, 0)),
            pl.BlockSpec((ATT_W, MRG_TN), lambda i, j: (0, j)),
            pl.BlockSpec((LRU_W, MRG_TN), lambda i, j: (0, j)),
            pl.BlockSpec((MRG_TM, MRG_TN), lambda i, j: (i, ga_blk + j)),
            pl.BlockSpec((MRG_TM, MRG_TN), lambda i, j: (i, gr_blk + j)),
        ],
        out_specs=pl.BlockSpec((MRG_TM, MRG_TN), lambda i, j: (i, j)),
        compiler_params=_params(2, 52),
        name="merge",
    )(att, lru, wa_bf, wl_bf, proj, proj)


RES_TM, RES_TN = 1024, 512


def _res_matmul_kernel(a_ref, w_ref, x_ref, o_ref):
    o_ref[...] = x_ref[...] + _dot(a_ref[...], w_ref[...])


def _res_matmul(a, w_bf, x2, name):
    t, k = a.shape
    n = w_bf.shape[1]
    return pl.pallas_call(
        _res_matmul_kernel,
        out_shape=jax.ShapeDtypeStruct((t, n), F32),
        grid=(t // RES_TM, n // RES_TN),
        in_specs=[
            pl.BlockSpec((RES_TM, k), lambda i, j: (i, 0)),
            pl.BlockSpec((k, RES_TN), lambda i, j: (0, j)),
            pl.BlockSpec((RES_TM, RES_TN), lambda i, j: (i, j)),
        ],
        out_specs=pl.BlockSpec((RES_TM, RES_TN), lambda i, j: (i, j)),
        compiler_params=_params(2, 52),
        name=name,
    )(a, w_bf, x2)


UP_TM, UP_TN, UP_SUB, UP_ROWS = 512, 1024, 256, 32
UP_CHUNK = UP_TM // V7X_SUBLANES


def _ffn_norm_kernel(x_ref, g_ref, h_ref, stage_ref):
    sub = V7X_SUBLANES
    g = g_ref[...]
    for m in range(UP_TM // sub):
        y = _rmsnorm(x_ref[m * sub:(m + 1) * sub, :], g)
        chunk, k0 = divmod(m * sub, UP_CHUNK)
        for cb in range(D_MODEL // V7X_LANES):
            stage_ref[cb, pl.ds(sub * k0 + chunk, sub, stride=sub), :] = (
                y[:, cb * V7X_LANES:(cb + 1) * V7X_LANES])
    for cb in range(D_MODEL // V7X_LANES):
        h_ref[:, cb * V7X_LANES:(cb + 1) * V7X_LANES] = stage_ref[cb].astype(BF16)


def _ffn_norm(x2, g):
    t = x2.shape[0]
    return pl.pallas_call(
        _ffn_norm_kernel,
        out_shape=jax.ShapeDtypeStruct((t, D_MODEL), BF16),
        grid=(t // UP_TM,),
        in_specs=[pl.BlockSpec((UP_TM, D_MODEL), lambda i: (i, 0)),
                  pl.BlockSpec((1, D_MODEL), lambda i: (0, 0))],
        out_specs=pl.BlockSpec((UP_TM, D_MODEL), lambda i: (i, 0)),
        scratch_shapes=[pltpu.VMEM((D_MODEL // V7X_LANES, UP_TM, V7X_LANES), F32)],
        compiler_params=_params(1, 40),
        name="ffn_norm",
    )(x2, g)


def _ffn_up_kernel(seq_tiles, h_ref, wg_ref, wv_ref, cwg_ref, cwv_ref, cbg_ref, cbv_ref,
                   o_ref, ext_g_ref, ext_v_ref, halo_g_ref, halo_v_ref):
    i, j = pl.program_id(0), pl.program_id(1)
    tm = UP_TM
    sub = V7X_SUBLANES
    pad = (FFN_CONV - 1) * sub

    @pl.when((i == 0) & (j == 0))
    def _():
        halo_g_ref[...] = jnp.zeros(halo_g_ref.shape, F32)
        halo_v_ref[...] = jnp.zeros(halo_v_ref.shape, F32)

    seq_start = i % seq_tiles == 0
    h = h_ref[...]
    first_row = lax.broadcasted_iota(jnp.int32, (sub, UP_SUB), 0) == 0

    def raw(w_ref, ext_ref, halo_ref, sl):
        ext_ref[pad:pad + tm, sl] = _dot(h, w_ref[:, sl])
        for m in range(FFN_CONV - 1):
            cur = ext_ref[tm + m * sub:tm + (m + 1) * sub, sl]
            old = jnp.where(seq_start, 0.0, halo_ref[j, m * sub:(m + 1) * sub, sl])
            ext_ref[m * sub:(m + 1) * sub, sl] = jnp.where(
                first_row, pltpu.roll(old, 1, axis=0), pltpu.roll(cur, 1, axis=0))
        halo_ref[j, :, sl] = ext_ref[tm:tm + pad, sl]

    def conv(ext_ref, cw, cb, r0, sl):
        y = ext_ref[r0:r0 + UP_ROWS, sl] * cw[0:1]
        for m in range(1, FFN_CONV):
            y = y + ext_ref[r0 + m * sub:r0 + m * sub + UP_ROWS, sl] * cw[m:m + 1]
        return y + cb

    for s in range(UP_TN // UP_SUB):
        sl = slice(s * UP_SUB, (s + 1) * UP_SUB)
        raw(wg_ref, ext_g_ref, halo_g_ref, sl)
        raw(wv_ref, ext_v_ref, halo_v_ref, sl)
        cwg, cwv, cbg, cbv = cwg_ref[:, sl], cwv_ref[:, sl], cbg_ref[:, sl], cbv_ref[:, sl]
        for r0 in range(0, tm, UP_ROWS):
            gate = _gelu_tanh(conv(ext_g_ref, cwg, cbg, r0, sl))
            val = conv(ext_v_ref, cwv, cbv, r0, sl)
            o_ref[r0:r0 + UP_ROWS, sl] = (gate * val).astype(BF16)


def _ffn_up(h, w_bf, cw, cb, seq):
    t = h.shape[0]
    nj = FFN_DIM // UP_TN
    pad = (FFN_CONV - 1) * V7X_SUBLANES
    assert seq % UP_TM == 0
    return pl.pallas_call(
        functools.partial(_ffn_up_kernel, seq // UP_TM),
        out_shape=jax.ShapeDtypeStruct((t, FFN_DIM), BF16),
        grid=(t // UP_TM, nj),
        in_specs=[
            pl.BlockSpec((UP_TM, D_MODEL), lambda i, j: (i, 0)),
            pl.BlockSpec((D_MODEL, UP_TN), lambda i, j: (0, j)),
            pl.BlockSpec((D_MODEL, UP_TN), lambda i, j: (0, nj + j)),
            pl.BlockSpec((FFN_CONV, UP_TN), lambda i, j: (0, j)),
            pl.BlockSpec((FFN_CONV, UP_TN), lambda i, j: (0, nj + j)),
            pl.BlockSpec((1, UP_TN), lambda i, j: (0, j)),
            pl.BlockSpec((1, UP_TN), lambda i, j: (0, nj + j)),
        ],
        out_specs=pl.BlockSpec((UP_TM, UP_TN), lambda i, j: (i, j)),
        scratch_shapes=[
            pltpu.VMEM((UP_TM + pad, UP_TN), F32),
            pltpu.VMEM((UP_TM + pad, UP_TN), F32),
            pltpu.VMEM((nj, pad, UP_TN), F32),
            pltpu.VMEM((nj, pad, UP_TN), F32),
        ],
        compiler_params=_params(2, 56),
        name="ffn_up",
    )(h, w_bf, w_bf, cw, cw, cb, cb)


DN_TM, DN_TN, DN_TK = 1024, 1024, 2048


def _ffn_down_kernel(a_ref, w_ref, x_ref, o_ref, acc_ref):
    k = pl.program_id(2)
    sub = V7X_SUBLANES
    lane_blocks = [slice(c * V7X_LANES, (c + 1) * V7X_LANES) for c in range(DN_TN // V7X_LANES)]

    @pl.when(k == 0)
    def _():
        d = _dot(a_ref[...], w_ref[...])
        for c, cols in enumerate(lane_blocks):
            acc_ref[c] = d[:, cols]

    @pl.when(k > 0)
    def _():
        d = _dot(a_ref[...], w_ref[...])
        for c, cols in enumerate(lane_blocks):
            acc_ref[c] += d[:, cols]

    @pl.when(k == pl.num_programs(2) - 1)
    def _():
        for blk in range(0, DN_TM, UP_TM):
            for m in range(UP_TM // sub):
                chunk, k0 = divmod(m * sub, UP_CHUNK)
                rows = slice(blk + m * sub, blk + (m + 1) * sub)
                src = pl.ds(blk + sub * k0 + chunk, sub, stride=sub)
                for c, cols in enumerate(lane_blocks):
                    o_ref[rows, cols] = x_ref[rows, cols] + acc_ref[c, src, :]


def _ffn_down(a, w_bf, x2):
    t, kk = a.shape
    n = w_bf.shape[1]
    return pl.pallas_call(
        _ffn_down_kernel,
        out_shape=jax.ShapeDtypeStruct((t, n), F32),
        grid=(t // DN_TM, n // DN_TN, kk // DN_TK),
        in_specs=[
            pl.BlockSpec((DN_TM, DN_TK), lambda i, j, k: (i, k)),
            pl.BlockSpec((DN_TK, DN_TN), lambda i, j, k: (k, j)),
            pl.BlockSpec((DN_TM, DN_TN), lambda i, j, k: (i, j)),
        ],
        out_specs=pl.BlockSpec((DN_TM, DN_TN), lambda i, j, k: (i, j)),
        scratch_shapes=[pltpu.VMEM((DN_TN // V7X_LANES, DN_TM, V7X_LANES), F32)],
        compiler_params=_params(3, 48),
        name="ffn_down",
    )(a, w_bf, x2)


PLE_TM, PLE_TN = 512, 512


def _ple_kernel(x_ref, p_ref, gg_ref, wg_ref, wp_ref, ge_ref, gf_ref, o_ref, h_ref):
    j = pl.program_id(1)

    @pl.when(j == 0)
    def _():
        _rmsnorm_rows(x_ref, gg_ref[...], h_ref)
        for c in range(0, D_MODEL, PLE_TN):
            o_ref[:, c:c + PLE_TN] = _dot(p_ref[...], wp_ref[:, c:c + PLE_TN])
        _rmsnorm_rows(o_ref, ge_ref[...], o_ref)

    col = pl.ds(pl.multiple_of(j * PLE_TN, PLE_TN), PLE_TN)
    gate = _sigmoid(_dot(h_ref[...], wg_ref[...]))
    o_ref[:, col] = x_ref[:, col] + gate * o_ref[:, col]

    @pl.when(j == pl.num_programs(1) - 1)
    def _():
        _rmsnorm_rows(o_ref, gf_ref[...], o_ref)


def _ple(x2, p_bf, gg, wg_bf, wp_bf, ge, gf):
    t = x2.shape[0]
    row = lambda: pl.BlockSpec((1, D_MODEL), lambda i, j: (0, 0))
    return pl.pallas_call(
        _ple_kernel,
        out_shape=jax.ShapeDtypeStruct((t, D_MODEL), F32),
        grid=(t // PLE_TM, D_MODEL // PLE_TN),
        in_specs=[
            pl.BlockSpec((PLE_TM, D_MODEL), lambda i, j: (i, 0)),
            pl.BlockSpec((PLE_TM, PLE_DIM), lambda i, j: (i, 0)),
            row(),
            pl.BlockSpec((D_MODEL, PLE_TN), lambda i, j: (0, j)),
            pl.BlockSpec((PLE_DIM, D_MODEL), lambda i, j: (0, 0)),
            row(),
            row(),
        ],
        out_specs=pl.BlockSpec((PLE_TM, D_MODEL), lambda i, j: (i, 0)),
        scratch_shapes=[pltpu.VMEM((PLE_TM, D_MODEL), BF16)],
        compiler_params=_params(2, 56),
        name="ple_final",
    )(x2, p_bf, gg, wg_bf, wp_bf, ge, gf)


def _group_major_qkv(w):
    cols = [w[:, base + g * GROUP_W:base + (g + 1) * GROUP_W]
            for g in range(N_GROUPS) for base in (0, ATT_W, 2 * ATT_W)]
    return jnp.concatenate(cols + [w[:, OFF_XR:]], axis=1)


def kernel(x, p, mix_norm_g, w_in, lru_conv_w, lru_conv_b, w_rgate, b_rgate, w_igate, b_igate, lru_lambda, w_att_branch, w_lru_branch, w_out, ffn_norm_g, w_up, ffn_conv_w, ffn_conv_b, w_down, ple_proj, ple_norm_g, ple_gate_norm_g, ple_gate_w, final_norm_g):
    b, s, d = x.shape
    depth = w_in.shape[0]
    t = b * s
    row = lambda v: v.reshape(1, -1)
    xf = x.reshape(t, d)
    for l in range(depth):
        proj = _in_proj(xf, row(mix_norm_g[l]), _group_major_qkv(w_in[l]).astype(BF16))
        proj3 = proj.reshape(b, s, IN_COLS)
        os, lses = zip(*[_attention_dense(proj3, g) if dil == 1 else _attention_strided(proj3, g, dil)
                         for g, (_, dil) in enumerate(PATTERNS)])
        att = _combine(os, lses)
        lru = _rglru(proj3, lru_conv_w[l], row(lru_conv_b[l]), w_rgate[l].astype(BF16), row(b_rgate[l]),
                     w_igate[l].astype(BF16), row(b_igate[l]), row(lru_lambda[l]))
        merged = _merge(att, lru, w_att_branch[l].astype(BF16), w_lru_branch[l].astype(BF16), proj)
        xf = _res_matmul(merged, w_out[l].astype(BF16), xf, "out_proj")
        hn = _ffn_norm(xf, row(ffn_norm_g[l]))
        gact = _ffn_up(hn, w_up[l].astype(BF16), ffn_conv_w[l], row(ffn_conv_b[l]), s)
        xf = _ffn_down(gact, w_down[l].astype(BF16), xf)
        assert l == depth - 1 == 0
        xf = _ple(xf, p[l].reshape(t, PLE_DIM).astype(BF16), row(ple_gate_norm_g[l]),
                  ple_gate_w[l].astype(BF16), ple_proj[l].astype(BF16), row(ple_norm_g[l]), row(final_norm_g))
    return xf.reshape(b, s, d)
```

```python
import functools
import math

import jax
import jax.numpy as jnp
from jax import lax
from jax.experimental import pallas as pl
from jax.experimental.pallas import tpu as pltpu

F32 = jnp.float32
BF16 = jnp.bfloat16

D_MODEL = 4096
HEAD_DIM = 128
HEADS_PER_GROUP = 8
PATTERNS = ((128, 1), (512, 4), (2048, 16))
N_GROUPS = len(PATTERNS)
GROUP_W = HEADS_PER_GROUP * HEAD_DIM
ATT_W = N_GROUPS * GROUP_W
ATT_BLK = 128
LRU_W = D_MODEL
LRU_BLOCKS = 16
LRU_BW = LRU_W // LRU_BLOCKS
LRU_C = 8.0
LRU_CONV = 4
FFN_DIM = 3 * D_MODEL
FFN_CONV = 3
PLE_DIM = 256
EPS = 1e-6
OFF_XR = 3 * ATT_W
OFF_YR = OFF_XR + LRU_W
OFF_GA = OFF_YR + LRU_W
OFF_GR = OFF_GA + D_MODEL
IN_COLS = OFF_GR + D_MODEL

V7X_VMEM_BYTES = 64 * 1024 * 1024
V7X_SUBLANES = 8
V7X_LANES = 128
MIB = 1024 * 1024


def _params(n_axes, vmem_mib):
    assert vmem_mib * MIB < V7X_VMEM_BYTES
    return pltpu.CompilerParams(
        dimension_semantics=("arbitrary",) * n_axes,
        vmem_limit_bytes=vmem_mib * MIB,
    )


def _sigmoid(v):
    return 1.0 / (1.0 + jnp.exp(-v))


def _gelu_tanh(v):
    c = math.sqrt(2.0 / math.pi)
    return v * (0.5 * (1.0 + jnp.tanh(c * (v + 0.044715 * (v * v * v)))))


def _rmsnorm(v, g):
    return v * lax.rsqrt(jnp.mean(v * v, axis=-1, keepdims=True) + EPS) * g


def _rmsnorm_rows(src_ref, g, dst_ref):
    rows = 2 * V7X_SUBLANES if dst_ref.dtype == BF16 else V7X_SUBLANES
    for r0 in range(0, src_ref.shape[0], rows):
        dst_ref[r0:r0 + rows, :] = _rmsnorm(src_ref[r0:r0 + rows, :], g).astype(dst_ref.dtype)


def _dot(a, b):
    return jnp.dot(a, b, preferred_element_type=F32)


IN_TM, IN_TN, IN_SUB = 512, 1024, 256


def _in_proj_kernel(x_ref, g_ref, w_ref, o_ref, h_ref):
    j = pl.program_id(1)

    @pl.when(j == 0)
    def _():
        _rmsnorm_rows(x_ref, g_ref[...], h_ref)

    def body(act):
        h = h_ref[...]
        for s in range(IN_TN // IN_SUB):
            sl = slice(s * IN_SUB, (s + 1) * IN_SUB)
            o_ref[:, sl] = act(_dot(h, w_ref[:, sl])).astype(BF16)

    j_yr, j_ga = OFF_YR // IN_TN, OFF_GA // IN_TN
    pl.when(j < j_yr)(functools.partial(body, lambda v: v))
    pl.when((j >= j_yr) & (j < j_ga))(functools.partial(body, _gelu_tanh))
    pl.when(j >= j_ga)(functools.partial(body, _sigmoid))


def _in_proj(x2, g, w_bf):
    t = x2.shape[0]
    assert OFF_YR % IN_TN == 0 and OFF_GA % IN_TN == 0
    return pl.pallas_call(
        _in_proj_kernel,
        out_shape=jax.ShapeDtypeStruct((t, IN_COLS), BF16),
        grid=(t // IN_TM, IN_COLS // IN_TN),
        in_specs=[
            pl.BlockSpec((IN_TM, D_MODEL), lambda i, j: (i, 0)),
            pl.BlockSpec((1, D_MODEL), lambda i, j: (0, 0)),
            pl.BlockSpec((D_MODEL, IN_TN), lambda i, j: (0, j)),
        ],
        out_specs=pl.BlockSpec((IN_TM, IN_TN), lambda i, j: (i, j)),
        scratch_shapes=[pltpu.VMEM((IN_TM, D_MODEL), BF16)],
        compiler_params=_params(2, 56),
        name="in_proj",
    )(x2, g, w_bf)


def _attend_heads(q, kc, kp, vc, vp, n_heads, has_prev, sc_ref, sp_ref, pc_ref, pp_ref):
    qi = lax.broadcasted_iota(jnp.int32, (ATT_BLK, ATT_BLK), 0)
    kj = lax.broadcasted_iota(jnp.int32, (ATT_BLK, ATT_BLK), 1)
    mask_c = kj <= qi
    mask_p = (kj >= qi) & has_prev
    neg = jnp.finfo(F32).min
    scale = HEAD_DIM ** -0.5
    nt = (((1,), (1,)), ((), ()))
    for h in range(n_heads):
        sc_ref[h] = lax.dot_general(q(h), kc(h), nt, preferred_element_type=F32)
        sp_ref[h] = lax.dot_general(q(h), kp(h), nt, preferred_element_type=F32)
    inv_den, lses = [], []
    for h in range(n_heads):
        s_c = jnp.where(mask_c, sc_ref[h] * scale, neg)
        s_p = jnp.where(mask_p, sp_ref[h] * scale, neg)
        mx = jnp.max(jnp.maximum(s_c, s_p), axis=-1, keepdims=True)
        p_c = jnp.exp(s_c - mx)
        p_p = jnp.exp(s_p - mx)
        den = jnp.sum(p_c + p_p, axis=-1, keepdims=True)
        pc_ref[h] = p_c.astype(BF16)
        pp_ref[h] = p_p.astype(BF16)
        inv_den.append(1.0 / den)
        lses.append(mx + jnp.log(den))
    outs = [(_dot(pc_ref[h], vc(h)) + _dot(pp_ref[h], vp(h))) * inv_den[h] for h in range(n_heads)]
    return outs, lses


def _lse_tile(lses, first_lane):
    lane = lax.broadcasted_iota(jnp.int32, (ATT_BLK, V7X_LANES), 1)
    tile = jnp.zeros((ATT_BLK, V7X_LANES), F32)
    for h, lse in enumerate(lses):
        tile = jnp.where(lane == first_lane + h, lse, tile)
    return tile


def _head_cols(h):
    return slice(h * HEAD_DIM, (h + 1) * HEAD_DIM)


def _attn_kernel(q_ref, kp_ref, kc_ref, vp_ref, vc_ref, o_ref, lse_ref, sc_ref, sp_ref, pc_ref, pp_ref):
    load = lambda ref: (lambda h: ref[:, _head_cols(h)])
    outs, lses = _attend_heads(load(q_ref), load(kc_ref), load(kp_ref), load(vc_ref), load(vp_ref),
                               HEADS_PER_GROUP, pl.program_id(1) > 0, sc_ref, sp_ref, pc_ref, pp_ref)
    lse_ref[...] = _lse_tile(lses, 0)
    for h, o in enumerate(outs):
        o_ref[:, _head_cols(h)] = o.astype(BF16)


def _score_scratch(n_heads):
    return ([pltpu.VMEM((n_heads, ATT_BLK, ATT_BLK), F32)] * 2
            + [pltpu.VMEM((n_heads, ATT_BLK, ATT_BLK), BF16)] * 2)


def _attention_dense(proj3, g):
    b, s, _ = proj3.shape
    assert s % ATT_BLK == 0

    def cur(which):
        return pl.BlockSpec((None, ATT_BLK, GROUP_W), lambda bi, c: (bi, c, which * N_GROUPS + g))

    def prev(which):
        return pl.BlockSpec((None, ATT_BLK, GROUP_W),
                            lambda bi, c: (bi, jnp.maximum(c - 1, 0), which * N_GROUPS + g))

    o, lse = pl.pallas_call(
        _attn_kernel,
        out_shape=(jax.ShapeDtypeStruct((b, s, GROUP_W), BF16),
                   jax.ShapeDtypeStruct((b, s, V7X_LANES), F32)),
        grid=(b, s // ATT_BLK),
        in_specs=[cur(0), prev(1), cur(1), prev(2), cur(2)],
        out_specs=(pl.BlockSpec((None, ATT_BLK, GROUP_W), lambda bi, c: (bi, c, 0)),
                   pl.BlockSpec((None, ATT_BLK, V7X_LANES), lambda bi, c: (bi, c, 0))),
        scratch_shapes=_score_scratch(HEADS_PER_GROUP),
        compiler_params=_params(2, 32),
        name=f"attn_g{g}",
    )(proj3, proj3, proj3, proj3, proj3)
    return o.reshape(b * s, GROUP_W), lse.reshape(b * s, V7X_LANES)


ATT_HW = 512
ATT_NH = ATT_HW // HEAD_DIM
ATT_RES = 2
ATT_PERM = 256


def _attn_strided_kernel(dil, q_ref, k_ref, v_ref, o_ref, lse_ref, qd_ref, kd_ref, vd_ref, od_ref, ls_ref,
                         sc_ref, sp_ref, pc_ref, pp_ref):
    c, half = pl.program_id(1), pl.program_id(2)
    slot = c % 2
    kc_ref, kp_ref = kd_ref.at[half, slot], kd_ref.at[half, 1 - slot]
    vc_ref, vp_ref = vd_ref.at[half, slot], vd_ref.at[half, 1 - slot]
    ch = ATT_BLK * dil
    n = ATT_PERM // dil
    row = lax.broadcasted_iota(jnp.int32, (ATT_PERM, ATT_PERM), 0)
    col = lax.broadcasted_iota(jnp.int32, (ATT_PERM, ATT_PERM), 1)
    to_residues = jnp.where(col == (row % n) * dil + row // n, 1.0, 0.0).astype(BF16)
    to_tokens = jnp.where(col == (row % dil) * n + row // dil, 1.0, 0.0).astype(BF16)

    @pl.when(c == 0)
    def _():
        kp_ref[...] = jnp.zeros(kp_ref.shape, BF16)
        vp_ref[...] = jnp.zeros(vp_ref.shape, BF16)

    def split(src_ref, dst_ref):
        for m in range(ch // ATT_PERM):
            y = _dot(to_residues, src_ref[m * ATT_PERM:(m + 1) * ATT_PERM, :]).astype(BF16)
            for r in range(dil):
                dst_ref[r, m * n:(m + 1) * n, :] = y[r * n:(r + 1) * n, :]

    split(q_ref, qd_ref)
    split(k_ref, kc_ref)
    split(v_ref, vc_ref)

    for r0 in range(0, dil, ATT_RES):
        load = lambda ref: (lambda v: ref[r0 + v // ATT_NH, :, _head_cols(v % ATT_NH)])
        outs, lses = _attend_heads(load(qd_ref), load(kc_ref), load(kp_ref), load(vc_ref), load(vp_ref),
                                   ATT_RES * ATT_NH, c > 0, sc_ref, sp_ref, pc_ref, pp_ref)
        for i in range(ATT_RES):
            ls_ref[pl.ds(r0 + i, ATT_BLK, stride=dil), :] = _lse_tile(
                lses[i * ATT_NH:(i + 1) * ATT_NH], half * ATT_NH)
            for h in range(ATT_NH):
                od_ref[r0 + i, :, _head_cols(h)] = outs[i * ATT_NH + h].astype(BF16)
    for m in range(ch // ATT_PERM):
        regrouped = jnp.concatenate([od_ref[r, m * n:(m + 1) * n, :] for r in range(dil)], axis=0)
        o_ref[m * ATT_PERM:(m + 1) * ATT_PERM, :] = _dot(to_tokens, regrouped).astype(BF16)

    @pl.when(half == 0)
    def _():
        lse_ref[...] = ls_ref[...]

    @pl.when(half > 0)
    def _():
        lse_ref[...] += ls_ref[...]


def _attention_strided(proj3, g, dil):
    b, s, _ = proj3.shape
    ch = ATT_BLK * dil
    halves = GROUP_W // ATT_HW
    assert s % ch == 0
    col = lambda which: (lambda bi, c, hf: (bi, c, (which * N_GROUPS + g) * halves + hf))
    blk = lambda which: pl.BlockSpec((None, ch, ATT_HW), col(which))
    split_kv = pltpu.VMEM((halves, 2, dil, ATT_BLK, ATT_HW), BF16)
    o, lse = pl.pallas_call(
        functools.partial(_attn_strided_kernel, dil),
        out_shape=(jax.ShapeDtypeStruct((b, s, GROUP_W), BF16),
                   jax.ShapeDtypeStruct((b, s, V7X_LANES), F32)),
        grid=(b, s // ch, halves),
        in_specs=[blk(0), blk(1), blk(2)],
        out_specs=(pl.BlockSpec((None, ch, ATT_HW), lambda bi, c, hf: (bi, c, hf)),
                   pl.BlockSpec((None, ch, V7X_LANES), lambda bi, c, hf: (bi, c, 0))),
        scratch_shapes=[
            pltpu.VMEM((dil, ATT_BLK, ATT_HW), BF16),
            split_kv,
            split_kv,
            pltpu.VMEM((dil, ATT_BLK, ATT_HW), BF16),
            pltpu.VMEM((ch, V7X_LANES), F32),
        ] + _score_scratch(ATT_RES * ATT_NH),
        compiler_params=_params(3, 56),
        name=f"attn_g{g}",
    )(proj3, proj3, proj3)
    return o.reshape(b * s, GROUP_W), lse.reshape(b * s, V7X_LANES)


CMB_TM = 512


def _combine_kernel(o0_ref, o1_ref, o2_ref, l0_ref, l1_ref, l2_ref, att_ref):
    l0, l1, l2 = l0_ref[...], l1_ref[...], l2_ref[...]
    m = jnp.maximum(jnp.maximum(l0, l1), l2)
    e0, e1, e2 = jnp.exp(l0 - m), jnp.exp(l1 - m), jnp.exp(l2 - m)
    den = e0 + e1 + e2
    for g, (o_ref, e) in enumerate(((o0_ref, e0), (o1_ref, e1), (o2_ref, e2))):
        wgt = e / den
        for h in range(HEADS_PER_GROUP):
            sl = slice(h * HEAD_DIM, (h + 1) * HEAD_DIM)
            val = o_ref[:, sl].astype(F32) * wgt[:, h:h + 1]
            att_ref[:, g * GROUP_W + h * HEAD_DIM:g * GROUP_W + (h + 1) * HEAD_DIM] = val.astype(BF16)


def _combine(os, lses):
    t = os[0].shape[0]
    o_spec = pl.BlockSpec((CMB_TM, GROUP_W), lambda i: (i, 0))
    l_spec = pl.BlockSpec((CMB_TM, V7X_LANES), lambda i: (i, 0))
    return pl.pallas_call(
        _combine_kernel,
        out_shape=jax.ShapeDtypeStruct((t, ATT_W), BF16),
        grid=(t // CMB_TM,),
        in_specs=[o_spec] * 3 + [l_spec] * 3,
        out_specs=pl.BlockSpec((CMB_TM, ATT_W), lambda i: (i, 0)),
        compiler_params=_params(1, 32),
        name="attn_combine",
    )(*os, *lses)


LRU_TT, LRU_CW, LRU_ROWS = 512, 512, 32
LRU_CHUNK = LRU_TT // V7X_SUBLANES
LRU_PAD = (LRU_CONV - 1) * V7X_SUBLANES


def _compose_over_sublanes(a, b, row):
    for s in (1, 2, 4):
        m = row >= s
        b = jnp.where(m, a * pltpu.roll(b, s, axis=0) + b, b)
        a = jnp.where(m, a * pltpu.roll(a, s, axis=0), a)
    return a, b


def _chunk_order_perms(tile):
    row = lax.broadcasted_iota(jnp.int32, (tile, tile), 0)
    col = lax.broadcasted_iota(jnp.int32, (tile, tile), 1)
    chunk = tile // V7X_SUBLANES
    to_chunks = col == (row % V7X_SUBLANES) * chunk + row // V7X_SUBLANES
    to_tokens = col == (row % chunk) * V7X_SUBLANES + row // chunk
    return to_chunks.astype(BF16), to_tokens.astype(BF16)


def _lru_kernel(xr_ref, gy_ref, cw_ref, cb_ref, wr_ref, br_ref, wi_ref, bi_ref, lam_ref, to_chunks_ref,
                to_tokens_ref, o_ref, ext_ref, halo_ref, carry_ref, xc_ref, xcb_ref, a_ref, b_ref):
    tt, sub, pad = LRU_TT, V7X_SUBLANES, LRU_PAD
    row = lax.broadcasted_iota(jnp.int32, (sub, LRU_CW), 0)

    @pl.when(pl.program_id(2) == 0)
    def _():
        halo_ref[...] = jnp.zeros(halo_ref.shape, F32)
        carry_ref[...] = jnp.zeros(carry_ref.shape, F32)

    ext_ref[pad:pad + tt, :] = _dot(to_chunks_ref[...], xr_ref[...])
    for m in range(LRU_CONV - 1):
        cur = ext_ref[tt + m * sub:tt + (m + 1) * sub, :]
        old = halo_ref[m * sub:(m + 1) * sub, :]
        ext_ref[m * sub:(m + 1) * sub, :] = jnp.where(
            row == 0, pltpu.roll(old, 1, axis=0), pltpu.roll(cur, 1, axis=0))
    halo_ref[...] = ext_ref[tt:tt + pad, :]

    cw = cw_ref[...]
    cb = cb_ref[...]
    for r0 in range(0, tt, LRU_ROWS):
        xc = ext_ref[r0:r0 + LRU_ROWS, :] * cw[0:1]
        for m in range(1, LRU_CONV):
            xc = xc + ext_ref[r0 + m * sub:r0 + m * sub + LRU_ROWS, :] * cw[m:m + 1]
        xc = xc + cb
        xc_ref[r0:r0 + LRU_ROWS, :] = xc
        xcb_ref[r0:r0 + LRU_ROWS, :] = xc.astype(BF16)

    for n in range(LRU_CW // LRU_BW):
        cols = slice(n * LRU_BW, (n + 1) * LRU_BW)
        a_ref[:, cols] = _dot(xcb_ref[:, cols], wr_ref[n])
        b_ref[:, cols] = _dot(xcb_ref[:, cols], wi_ref[n])

    lam = lam_ref[...]
    neg_c_softplus = -LRU_C * (jnp.maximum(-lam, 0.0) + jnp.log1p(jnp.exp(-jnp.abs(lam))))
    br, bi = br_ref[...], bi_ref[...]
    for r0 in range(0, tt, LRU_ROWS):
        rows = slice(r0, r0 + LRU_ROWS)
        r = _sigmoid(a_ref[rows, :] + br)
        i = _sigmoid(b_ref[rows, :] + bi)
        log_a = r * neg_c_softplus
        a = jnp.exp(log_a)
        a_ref[rows, :] = a
        b_ref[rows, :] = jnp.sqrt(-jnp.tanh(log_a) * (a * a + 1.0)) * i * xc_ref[rows, :]

    h_end = b_ref[0:sub, :]
    decay = a_ref[0:sub, :]
    for k in range(1, LRU_CHUNK):
        av = a_ref[k * sub:(k + 1) * sub, :]
        h_end = av * h_end + b_ref[k * sub:(k + 1) * sub, :]
        decay = av * decay
    pa, pb = _compose_over_sublanes(decay, h_end, row)
    h_in = carry_ref[...]
    h_out = pa * h_in + pb
    carry_ref[...] = jnp.broadcast_to(h_out[sub - 1:sub, :], (sub, LRU_CW))
    h = jnp.where(row == 0, h_in, pltpu.roll(h_out, 1, axis=0))
    for k in range(LRU_CHUNK):
        h = a_ref[k * sub:(k + 1) * sub, :] * h + b_ref[k * sub:(k + 1) * sub, :]
        xc_ref[k * sub:(k + 1) * sub, :] = h
    gated = (xc_ref[...] * _dot(to_chunks_ref[...], gy_ref[...])).astype(BF16)
    o_ref[...] = _dot(to_tokens_ref[...], gated).astype(BF16)


def _rglru(proj3, cw, cb, wr_bf, br, wi_bf, bi, lam):
    b, s, _ = proj3.shape
    assert OFF_XR % LRU_CW == 0 and OFF_YR % LRU_CW == 0 and s % LRU_TT == 0
    xr_blk, gy_blk = OFF_XR // LRU_CW, OFF_YR // LRU_CW
    gate_blocks = LRU_CW // LRU_BW
    vec = lambda: pl.BlockSpec((1, LRU_CW), lambda n, bi_, t: (0, n))
    gate_w = lambda: pl.BlockSpec((gate_blocks, LRU_BW, LRU_BW), lambda n, bi_, t: (n, 0, 0))
    perm = lambda: pl.BlockSpec((LRU_TT, LRU_TT), lambda n, bi_, t: (0, 0))
    to_chunks, to_tokens = _chunk_order_perms(LRU_TT)
    out = pl.pallas_call(
        _lru_kernel,
        out_shape=jax.ShapeDtypeStruct((b, s, LRU_W), BF16),
        grid=(LRU_W // LRU_CW, b, s // LRU_TT),
        in_specs=[
            pl.BlockSpec((None, LRU_TT, LRU_CW), lambda n, bi_, t: (bi_, t, xr_blk + n)),
            pl.BlockSpec((None, LRU_TT, LRU_CW), lambda n, bi_, t: (bi_, t, gy_blk + n)),
            pl.BlockSpec((LRU_CONV, LRU_CW), lambda n, bi_, t: (0, n)),
            vec(),
            gate_w(),
            vec(),
            gate_w(),
            vec(),
            vec(),
            perm(),
            perm(),
        ],
        out_specs=pl.BlockSpec((None, LRU_TT, LRU_CW), lambda n, bi_, t: (bi_, t, n)),
        scratch_shapes=[
            pltpu.VMEM((LRU_TT + LRU_PAD, LRU_CW), F32),
            pltpu.VMEM((LRU_PAD, LRU_CW), F32),
            pltpu.VMEM((V7X_SUBLANES, LRU_CW), F32),
            pltpu.VMEM((LRU_TT, LRU_CW), F32),
            pltpu.VMEM((LRU_TT, LRU_CW), BF16),
            pltpu.VMEM((LRU_TT, LRU_CW), F32),
            pltpu.VMEM((LRU_TT, LRU_CW), F32),
        ],
        compiler_params=_params(3, 32),
        name="rglru",
    )(proj3, proj3, cw, cb, wr_bf, br, wi_bf, bi, lam, to_chunks, to_tokens)
    return out.reshape(b * s, LRU_W)


MRG_TM, MRG_TN = 1024, 512


def _merge_kernel(att_ref, lru_ref, wa_ref, wl_ref, sa_ref, sr_ref, o_ref):
    ya = _dot(att_ref[...], wa_ref[...])
    yl = _dot(lru_ref[...], wl_ref[...])
    o_ref[...] = (sa_ref[...].astype(F32) * ya + sr_ref[...].astype(F32) * yl).astype(BF16)


def _merge(att, lru, wa_bf, wl_bf, proj):
    t = att.shape[0]
    ga_blk, gr_blk = OFF_GA // MRG_TN, OFF_GR // MRG_TN
    return pl.pallas_call(
        _merge_kernel,
        out_shape=jax.ShapeDtypeStruct((t, D_MODEL), BF16),
        grid=(t // MRG_TM, D_MODEL // MRG_TN),
        in_specs=[
            pl.BlockSpec((MRG_TM, ATT_W), lambda i, j: (i, 0)),
            pl.BlockSpec((MRG_TM, LRU_W), lambda i, j: (i, 0)),
            pl.BlockSpec((ATT_W, MRG_TN), lambda i, j: (0, j)),
            pl.BlockSpec((LRU_W, MRG_TN), lambda i, j: (0, j)),
            pl.BlockSpec((MRG_TM, MRG_TN), lambda i, j: (i, ga_blk + j)),
            pl.BlockSpec((MRG_TM, MRG_TN), lambda i, j: (i, gr_blk + j)),
        ],
        out_specs=pl.BlockSpec((MRG_TM, MRG_TN), lambda i, j: (i, j)),
        compiler_params=_params(2, 52),
        name="merge",
    )(att, lru, wa_bf, wl_bf, proj, proj)


RES_TM, RES_TN = 1024, 512


def _res_matmul_kernel(a_ref, w_ref, x_ref, o_ref):
    o_ref[...] = x_ref[...] + _dot(a_ref[...], w_ref[...])


def _res_matmul(a, w_bf, x2, name):
    t, k = a.shape
    n = w_bf.shape[1]
    return pl.pallas_call(
        _res_matmul_kernel,
        out_shape=jax.ShapeDtypeStruct((t, n), F32),
        grid=(t // RES_TM, n // RES_TN),
        in_specs=[
            pl.BlockSpec((RES_TM, k), lambda i, j: (i, 0)),
            pl.BlockSpec((k, RES_TN), lambda i, j: (0, j)),
            pl.BlockSpec((RES_TM, RES_TN), lambda i, j: (i, j)),
        ],
        out_specs=pl.BlockSpec((RES_TM, RES_TN), lambda i, j: (i, j)),
        compiler_params=_params(2, 52),
        name=name,
    )(a, w_bf, x2)


UP_TM, UP_TN, UP_SUB, UP_ROWS = 512, 1024, 256, 32
UP_CHUNK = UP_TM // V7X_SUBLANES


def _ffn_norm_kernel(x_ref, g_ref, h_ref, stage_ref):
    sub = V7X_SUBLANES
    g = g_ref[...]
    for m in range(UP_TM // sub):
        y = _rmsnorm(x_ref[m * sub:(m + 1) * sub, :], g)
        chunk, k0 = divmod(m * sub, UP_CHUNK)
        for cb in range(D_MODEL // V7X_LANES):
            stage_ref[cb, pl.ds(sub * k0 + chunk, sub, stride=sub), :] = (
                y[:, cb * V7X_LANES:(cb + 1) * V7X_LANES])
    for cb in range(D_MODEL // V7X_LANES):
        h_ref[:, cb * V7X_LANES:(cb + 1) * V7X_LANES] = stage_ref[cb].astype(BF16)


def _ffn_norm(x2, g):
    t = x2.shape[0]
    return pl.pallas_call(
        _ffn_norm_kernel,
        out_shape=jax.ShapeDtypeStruct((t, D_MODEL), BF16),
        grid=(t // UP_TM,),
        in_specs=[pl.BlockSpec((UP_TM, D_MODEL), lambda i: (i, 0)),
                  pl.BlockSpec((1, D_MODEL), lambda i: (0, 0))],
        out_specs=pl.BlockSpec((UP_TM, D_MODEL), lambda i: (i, 0)),
        scratch_shapes=[pltpu.VMEM((D_MODEL // V7X_LANES, UP_TM, V7X_LANES), F32)],
        compiler_params=_params(1, 40),
        name="ffn_norm",
    )(x2, g)


def _ffn_up_kernel(seq_tiles, h_ref, wg_ref, wv_ref, cwg_ref, cwv_ref, cbg_ref, cbv_ref,
                   o_ref, ext_g_ref, ext_v_ref, halo_g_ref, halo_v_ref):
    i, j = pl.program_id(0), pl.program_id(1)
    tm = UP_TM
    sub = V7X_SUBLANES
    pad = (FFN_CONV - 1) * sub

    @pl.when((i == 0) & (j == 0))
    def _():
        halo_g_ref[...] = jnp.zeros(halo_g_ref.shape, F32)
        halo_v_ref[...] = jnp.zeros(halo_v_ref.shape, F32)

    seq_start = i % seq_tiles == 0
    h = h_ref[...]
    first_row = lax.broadcasted_iota(jnp.int32, (sub, UP_SUB), 0) == 0

    def raw(w_ref, ext_ref, halo_ref, sl):
        ext_ref[pad:pad + tm, sl] = _dot(h, w_ref[:, sl])
        for m in range(FFN_CONV - 1):
            cur = ext_ref[tm + m * sub:tm + (m + 1) * sub, sl]
            old = jnp.where(seq_start, 0.0, halo_ref[j, m * sub:(m + 1) * sub, sl])
            ext_ref[m * sub:(m + 1) * sub, sl] = jnp.where(
                first_row, pltpu.roll(old, 1, axis=0), pltpu.roll(cur, 1, axis=0))
        halo_ref[j, :, sl] = ext_ref[tm:tm + pad, sl]

    def conv(ext_ref, cw, cb, r0, sl):
        y = ext_ref[r0:r0 + UP_ROWS, sl] * cw[0:1]
        for m in range(1, FFN_CONV):
            y = y + ext_ref[r0 + m * sub:r0 + m * sub + UP_ROWS, sl] * cw[m:m + 1]
        return y + cb

    for s in range(UP_TN // UP_SUB):
        sl = slice(s * UP_SUB, (s + 1) * UP_SUB)
        raw(wg_ref, ext_g_ref, halo_g_ref, sl)
        raw(wv_ref, ext_v_ref, halo_v_ref, sl)
        cwg, cwv, cbg, cbv = cwg_ref[:, sl], cwv_ref[:, sl], cbg_ref[:, sl], cbv_ref[:, sl]
        for r0 in range(0, tm, UP_ROWS):
            gate = _gelu_tanh(conv(ext_g_ref, cwg, cbg, r0, sl))
            val = conv(ext_v_ref, cwv, cbv, r0, sl)
            o_ref[r0:r0 + UP_ROWS, sl] = (gate * val).astype(BF16)


def _ffn_up(h, w_bf, cw, cb, seq):
    t = h.shape[0]
    nj = FFN_DIM // UP_TN
    pad = (FFN_CONV - 1) * V7X_SUBLANES
    assert seq % UP_TM == 0
    return pl.pallas_call(
        functools.partial(_ffn_up_kernel, seq // UP_TM),
        out_shape=jax.ShapeDtypeStruct((t, FFN_DIM), BF16),
        grid=(t // UP_TM, nj),
        in_specs=[
            pl.BlockSpec((UP_TM, D_MODEL), lambda i, j: (i, 0)),
            pl.BlockSpec((D_MODEL, UP_TN), lambda i, j: (0, j)),
            pl.BlockSpec((D_MODEL, UP_TN), lambda i, j: (0, nj + j)),
            pl.BlockSpec((FFN_CONV, UP_TN), lambda i, j: (0, j)),
            pl.BlockSpec((FFN_CONV, UP_TN), lambda i, j: (0, nj + j)),
            pl.BlockSpec((1, UP_TN), lambda i, j: (0, j)),
            pl.BlockSpec((1, UP_TN), lambda i, j: (0, nj + j)),
        ],
        out_specs=pl.BlockSpec((UP_TM, UP_TN), lambda i, j: (i, j)),
        scratch_shapes=[
            pltpu.VMEM((UP_TM + pad, UP_TN), F32),
            pltpu.VMEM((UP_TM + pad, UP_TN), F32),
            pltpu.VMEM((nj, pad, UP_TN), F32),
            pltpu.VMEM((nj, pad, UP_TN), F32),
        ],
        compiler_params=_params(2, 56),
        name="ffn_up",
    )(h, w_bf, w_bf, cw, cw, cb, cb)


DN_TN, DN_SUB = 512, 256


def _ffn_down_kernel(a_ref, w_ref, x_ref, o_ref, acc_ref):
    sub = V7X_SUBLANES
    a = a_ref[...]
    for s in range(0, DN_TN, DN_SUB):
        d = _dot(a, w_ref[:, s:s + DN_SUB])
        lane_blocks = [(c // V7X_LANES, slice(c, c + V7X_LANES)) for c in range(s, s + DN_SUB, V7X_LANES)]
        for c, cols in lane_blocks:
            acc_ref[c] = d[:, cols.start - s:cols.stop - s]
        for m in range(UP_TM // sub):
            chunk, k0 = divmod(m * sub, UP_CHUNK)
            rows = slice(m * sub, (m + 1) * sub)
            src = pl.ds(sub * k0 + chunk, sub, stride=sub)
            for c, cols in lane_blocks:
                o_ref[rows, cols] = x_ref[rows, cols] + acc_ref[c, src, :]


def _ffn_down(a, w_bf, x2):
    t, kk = a.shape
    n = w_bf.shape[1]
    return pl.pallas_call(
        _ffn_down_kernel,
        out_shape=jax.ShapeDtypeStruct((t, n), F32),
        grid=(t // UP_TM, n // DN_TN),
        in_specs=[
            pl.BlockSpec((UP_TM, kk), lambda i, j: (i, 0)),
            pl.BlockSpec((kk, DN_TN), lambda i, j: (0, j)),
            pl.BlockSpec((UP_TM, DN_TN), lambda i, j: (i, j)),
        ],
        out_specs=pl.BlockSpec((UP_TM, DN_TN), lambda i, j: (i, j)),
        scratch_shapes=[pltpu.VMEM((DN_TN // V7X_LANES, UP_TM, V7X_LANES), F32)],
        compiler_params=_params(2, 56),
        name="ffn_down",
    )(a, w_bf, x2)


PLE_TM, PLE_TN = 512, 512


def _ple_kernel(x_ref, p_ref, gg_ref, wg_ref, wp_ref, ge_ref, gf_ref, o_ref, h_ref):
    j = pl.program_id(1)

    @pl.when(j == 0)
    def _():
        _rmsnorm_rows(x_ref, gg_ref[...], h_ref)
        for c in range(0, D_MODEL, PLE_TN):
            o_ref[:, c:c + PLE_TN] = _dot(p_ref[...], wp_ref[:, c:c + PLE_TN])
        _rmsnorm_rows(o_ref, ge_ref[...], o_ref)

    col = pl.ds(pl.multiple_of(j * PLE_TN, PLE_TN), PLE_TN)
    gate = _sigmoid(_dot(h_ref[...], wg_ref[...]))
    o_ref[:, col] = x_ref[:, col] + gate * o_ref[:, col]

    @pl.when(j == pl.num_programs(1) - 1)
    def _():
        _rmsnorm_rows(o_ref, gf_ref[...], o_ref)


def _ple(x2, p_bf, gg, wg_bf, wp_bf, ge, gf):
    t = x2.shape[0]
    row = lambda: pl.BlockSpec((1, D_MODEL), lambda i, j: (0, 0))
    return pl.pallas_call(
        _ple_kernel,
        out_shape=jax.ShapeDtypeStruct((t, D_MODEL), F32),
        grid=(t // PLE_TM, D_MODEL // PLE_TN),
        in_specs=[
            pl.BlockSpec((PLE_TM, D_MODEL), lambda i, j: (i, 0)),
            pl.BlockSpec((PLE_TM, PLE_DIM), lambda i, j: (i, 0)),
            row(),
            pl.BlockSpec((D_MODEL, PLE_TN), lambda i, j: (0, j)),
            pl.BlockSpec((PLE_DIM, D_MODEL), lambda i, j: (0, 0)),
            row(),
            row(),
        ],
        out_specs=pl.BlockSpec((PLE_TM, D_MODEL), lambda i, j: (i, 0)),
        scratch_shapes=[pltpu.VMEM((PLE_TM, D_MODEL), BF16)],
        compiler_params=_params(2, 56),
        name="ple_final",
    )(x2, p_bf, gg, wg_bf, wp_bf, ge, gf)


def kernel(x, p, mix_norm_g, w_in, lru_conv_w, lru_conv_b, w_rgate, b_rgate, w_igate, b_igate, lru_lambda, w_att_branch, w_lru_branch, w_out, ffn_norm_g, w_up, ffn_conv_w, ffn_conv_b, w_down, ple_proj, ple_norm_g, ple_gate_norm_g, ple_gate_w, final_norm_g):
    b, s, d = x.shape
    depth = w_in.shape[0]
    t = b * s
    row = lambda v: v.reshape(1, -1)
    xf = x.reshape(t, d)
    for l in range(depth):
        proj = _in_proj(xf, row(mix_norm_g[l]), w_in[l].astype(BF16))
        proj3 = proj.reshape(b, s, IN_COLS)
        os, lses = zip(*[_attention_dense(proj3, g) if dil == 1 else _attention_strided(proj3, g, dil)
                         for g, (_, dil) in enumerate(PATTERNS)])
        att = _combine(os, lses)
        lru = _rglru(proj3, lru_conv_w[l], row(lru_conv_b[l]), w_rgate[l].astype(BF16), row(b_rgate[l]),
                     w_igate[l].astype(BF16), row(b_igate[l]), row(lru_lambda[l]))
        merged = _merge(att, lru, w_att_branch[l].astype(BF16), w_lru_branch[l].astype(BF16), proj)
        xf = _res_matmul(merged, w_out[l].astype(BF16), xf, "out_proj")
        hn = _ffn_norm(xf, row(ffn_norm_g[l]))
        gact = _ffn_up(hn, w_up[l].astype(BF16), ffn_conv_w[l], row(ffn_conv_b[l]), s)
        xf = _ffn_down(gact, w_down[l].astype(BF16), xf)
        assert l == depth - 1 == 0
        xf = _ple(xf, p[l].reshape(t, PLE_DIM).astype(BF16), row(ple_gate_norm_g[l]),
                  ple_gate_w[l].astype(BF16), ple_proj[l].astype(BF16), row(ple_norm_g[l]), row(final_norm_g))
    return xf.reshape(b, s, d)
```

```python
import functools
import math

import jax
import jax.numpy as jnp
from jax import lax
from jax.experimental import pallas as pl
from jax.experimental.pallas import tpu as pltpu

F32 = jnp.float32
BF16 = jnp.bfloat16

D_MODEL = 4096
HEAD_DIM = 128
HEADS_PER_GROUP = 8
PATTERNS = ((128, 1), (512, 4), (2048, 16))
N_GROUPS = len(PATTERNS)
GROUP_W = HEADS_PER_GROUP * HEAD_DIM
ATT_W = N_GROUPS * GROUP_W
ATT_BLK = 128
LRU_W = D_MODEL
LRU_BLOCKS = 16
LRU_BW = LRU_W // LRU_BLOCKS
LRU_C = 8.0
LRU_CONV = 4
FFN_DIM = 3 * D_MODEL
FFN_CONV = 3
PLE_DIM = 256
EPS = 1e-6
OFF_XR = 3 * ATT_W
OFF_YR = OFF_XR + LRU_W
OFF_GA = OFF_YR + LRU_W
OFF_GR = OFF_GA + D_MODEL
IN_COLS = OFF_GR + D_MODEL

V7X_VMEM_BYTES = 64 * 1024 * 1024
V7X_SUBLANES = 8
V7X_LANES = 128
MIB = 1024 * 1024


def _params(n_axes, vmem_mib):
    assert vmem_mib * MIB < V7X_VMEM_BYTES
    return pltpu.CompilerParams(
        dimension_semantics=("arbitrary",) * n_axes,
        vmem_limit_bytes=vmem_mib * MIB,
    )


def _sigmoid(v):
    return 1.0 / (1.0 + jnp.exp(-v))


def _gelu_tanh(v):
    c = math.sqrt(2.0 / math.pi)
    return v * (0.5 * (1.0 + jnp.tanh(c * (v + 0.044715 * (v * v * v)))))


def _rmsnorm(v, g):
    return v * lax.rsqrt(jnp.mean(v * v, axis=-1, keepdims=True) + EPS) * g


def _rmsnorm_rows(src_ref, g, dst_ref):
    rows = 2 * V7X_SUBLANES if dst_ref.dtype == BF16 else V7X_SUBLANES
    for r0 in range(0, src_ref.shape[0], rows):
        dst_ref[r0:r0 + rows, :] = _rmsnorm(src_ref[r0:r0 + rows, :], g).astype(dst_ref.dtype)


def _dot(a, b):
    return jnp.dot(a, b, preferred_element_type=F32)


IN_TM, IN_TN, IN_SUB = 512, 1024, 256


def _in_proj_kernel(x_ref, g_ref, w_ref, o_ref, h_ref):
    j = pl.program_id(1)

    @pl.when(j == 0)
    def _():
        _rmsnorm_rows(x_ref, g_ref[...], h_ref)

    def body(act):
        h = h_ref[...]
        for s in range(IN_TN // IN_SUB):
            sl = slice(s * IN_SUB, (s + 1) * IN_SUB)
            o_ref[:, sl] = act(_dot(h, w_ref[:, sl])).astype(BF16)

    j_yr, j_ga = OFF_YR // IN_TN, OFF_GA // IN_TN
    pl.when(j < j_yr)(functools.partial(body, lambda v: v))
    pl.when((j >= j_yr) & (j < j_ga))(functools.partial(body, _gelu_tanh))
    pl.when(j >= j_ga)(functools.partial(body, _sigmoid))


def _in_proj(x2, g, w_bf):
    t = x2.shape[0]
    assert OFF_YR % IN_TN == 0 and OFF_GA % IN_TN == 0
    return pl.pallas_call(
        _in_proj_kernel,
        out_shape=jax.ShapeDtypeStruct((t, IN_COLS), BF16),
        grid=(t // IN_TM, IN_COLS // IN_TN),
        in_specs=[
            pl.BlockSpec((IN_TM, D_MODEL), lambda i, j: (i, 0)),
            pl.BlockSpec((1, D_MODEL), lambda i, j: (0, 0)),
            pl.BlockSpec((D_MODEL, IN_TN), lambda i, j: (0, j)),
        ],
        out_specs=pl.BlockSpec((IN_TM, IN_TN), lambda i, j: (i, j)),
        scratch_shapes=[pltpu.VMEM((IN_TM, D_MODEL), BF16)],
        compiler_params=_params(2, 56),
        name="in_proj",
    )(x2, g, w_bf)


def _attend_heads(q, kc, kp, vc, vp, n_heads, has_prev, sc_ref, sp_ref, pc_ref, pp_ref):
    qi = lax.broadcasted_iota(jnp.int32, (ATT_BLK, ATT_BLK), 0)
    kj = lax.broadcasted_iota(jnp.int32, (ATT_BLK, ATT_BLK), 1)
    mask_c = kj <= qi
    mask_p = (kj >= qi) & has_prev
    neg = jnp.finfo(F32).min
    scale = HEAD_DIM ** -0.5
    nt = (((1,), (1,)), ((), ()))
    for h in range(n_heads):
        sc_ref[h] = lax.dot_general(q(h), kc(h), nt, preferred_element_type=F32)
        sp_ref[h] = lax.dot_general(q(h), kp(h), nt, preferred_element_type=F32)
    inv_den, lses = [], []
    for h in range(n_heads):
        s_c = jnp.where(mask_c, sc_ref[h] * scale, neg)
        s_p = jnp.where(mask_p, sp_ref[h] * scale, neg)
        mx = jnp.max(jnp.maximum(s_c, s_p), axis=-1, keepdims=True)
        p_c = jnp.exp(s_c - mx)
        p_p = jnp.exp(s_p - mx)
        den = jnp.sum(p_c + p_p, axis=-1, keepdims=True)
        pc_ref[h] = p_c.astype(BF16)
        pp_ref[h] = p_p.astype(BF16)
        inv_den.append(1.0 / den)
        lses.append(mx + jnp.log(den))
    outs = [(_dot(pc_ref[h], vc(h)) + _dot(pp_ref[h], vp(h))) * inv_den[h] for h in range(n_heads)]
    return outs, lses


def _lse_tile(lses, first_lane):
    lane = lax.broadcasted_iota(jnp.int32, (ATT_BLK, V7X_LANES), 1)
    tile = jnp.zeros((ATT_BLK, V7X_LANES), F32)
    for h, lse in enumerate(lses):
        tile = jnp.where(lane == first_lane + h, lse, tile)
    return tile


def _head_cols(h):
    return slice(h * HEAD_DIM, (h + 1) * HEAD_DIM)


def _attn_kernel(q_ref, kp_ref, kc_ref, vp_ref, vc_ref, o_ref, lse_ref, sc_ref, sp_ref, pc_ref, pp_ref):
    load = lambda ref: (lambda h: ref[:, _head_cols(h)])
    outs, lses = _attend_heads(load(q_ref), load(kc_ref), load(kp_ref), load(vc_ref), load(vp_ref),
                               HEADS_PER_GROUP, pl.program_id(1) > 0, sc_ref, sp_ref, pc_ref, pp_ref)
    lse_ref[...] = _lse_tile(lses, 0)
    for h, o in enumerate(outs):
        o_ref[:, _head_cols(h)] = o.astype(BF16)


def _score_scratch(n_heads):
    return ([pltpu.VMEM((n_heads, ATT_BLK, ATT_BLK), F32)] * 2
            + [pltpu.VMEM((n_heads, ATT_BLK, ATT_BLK), BF16)] * 2)


def _attention_dense(proj3, g):
    b, s, _ = proj3.shape
    assert s % ATT_BLK == 0

    def cur(which):
        return pl.BlockSpec((None, ATT_BLK, GROUP_W), lambda bi, c: (bi, c, which * N_GROUPS + g))

    def prev(which):
        return pl.BlockSpec((None, ATT_BLK, GROUP_W),
                            lambda bi, c: (bi, jnp.maximum(c - 1, 0), which * N_GROUPS + g))

    o, lse = pl.pallas_call(
        _attn_kernel,
        out_shape=(jax.ShapeDtypeStruct((b, s, GROUP_W), BF16),
                   jax.ShapeDtypeStruct((b, s, V7X_LANES), F32)),
        grid=(b, s // ATT_BLK),
        in_specs=[cur(0), prev(1), cur(1), prev(2), cur(2)],
        out_specs=(pl.BlockSpec((None, ATT_BLK, GROUP_W), lambda bi, c: (bi, c, 0)),
                   pl.BlockSpec((None, ATT_BLK, V7X_LANES), lambda bi, c: (bi, c, 0))),
        scratch_shapes=_score_scratch(HEADS_PER_GROUP),
        compiler_params=_params(2, 32),
        name=f"attn_g{g}",
    )(proj3, proj3, proj3, proj3, proj3)
    return o.reshape(b * s, GROUP_W), lse.reshape(b * s, V7X_LANES)


ATT_HW = 512
ATT_NH = ATT_HW // HEAD_DIM
ATT_RES = 2
ATT_PERM = 256


def _attn_strided_kernel(dil, q_ref, k_ref, v_ref, o_ref, lse_ref, qd_ref, kd_ref, vd_ref, od_ref, ls_ref,
                         sc_ref, sp_ref, pc_ref, pp_ref):
    c, half = pl.program_id(1), pl.program_id(2)
    slot = c % 2
    kc_ref, kp_ref = kd_ref.at[half, slot], kd_ref.at[half, 1 - slot]
    vc_ref, vp_ref = vd_ref.at[half, slot], vd_ref.at[half, 1 - slot]
    ch = ATT_BLK * dil
    n = ATT_PERM // dil
    row = lax.broadcasted_iota(jnp.int32, (ATT_PERM, ATT_PERM), 0)
    col = lax.broadcasted_iota(jnp.int32, (ATT_PERM, ATT_PERM), 1)
    to_residues = jnp.where(col == (row % n) * dil + row // n, 1.0, 0.0).astype(BF16)
    to_tokens = jnp.where(col == (row % dil) * n + row // dil, 1.0, 0.0).astype(BF16)

    @pl.when(c == 0)
    def _():
        kp_ref[...] = jnp.zeros(kp_ref.shape, BF16)
        vp_ref[...] = jnp.zeros(vp_ref.shape, BF16)

    def split(src_ref, dst_ref):
        for m in range(ch // ATT_PERM):
            y = _dot(to_residues, src_ref[m * ATT_PERM:(m + 1) * ATT_PERM, :]).astype(BF16)
            for r in range(dil):
                dst_ref[r, m * n:(m + 1) * n, :] = y[r * n:(r + 1) * n, :]

    split(q_ref, qd_ref)
    split(k_ref, kc_ref)
    split(v_ref, vc_ref)

    for r0 in range(0, dil, ATT_RES):
        load = lambda ref: (lambda v: ref[r0 + v // ATT_NH, :, _head_cols(v % ATT_NH)])
        outs, lses = _attend_heads(load(qd_ref), load(kc_ref), load(kp_ref), load(vc_ref), load(vp_ref),
                                   ATT_RES * ATT_NH, c > 0, sc_ref, sp_ref, pc_ref, pp_ref)
        for i in range(ATT_RES):
            ls_ref[pl.ds(r0 + i, ATT_BLK, stride=dil), :] = _lse_tile(
                lses[i * ATT_NH:(i + 1) * ATT_NH], half * ATT_NH)
            for h in range(ATT_NH):
                od_ref[r0 + i, :, _head_cols(h)] = outs[i * ATT_NH + h].astype(BF16)
    for m in range(ch // ATT_PERM):
        regrouped = jnp.concatenate([od_ref[r, m * n:(m + 1) * n, :] for r in range(dil)], axis=0)
        o_ref[m * ATT_PERM:(m + 1) * ATT_PERM, :] = _dot(to_tokens, regrouped).astype(BF16)

    @pl.when(half == 0)
    def _():
        lse_ref[...] = ls_ref[...]

    @pl.when(half > 0)
    def _():
        lse_ref[...] += ls_ref[...]


def _attention_strided(proj3, g, dil):
    b, s, _ = proj3.shape
    ch = ATT_BLK * dil
    halves = GROUP_W // ATT_HW
    assert s % ch == 0
    col = lambda which: (lambda bi, c, hf: (bi, c, (which * N_GROUPS + g) * halves + hf))
    blk = lambda which: pl.BlockSpec((None, ch, ATT_HW), col(which))
    split_kv = pltpu.VMEM((halves, 2, dil, ATT_BLK, ATT_HW), BF16)
    o, lse = pl.pallas_call(
        functools.partial(_attn_strided_kernel, dil),
        out_shape=(jax.ShapeDtypeStruct((b, s, GROUP_W), BF16),
                   jax.ShapeDtypeStruct((b, s, V7X_LANES), F32)),
        grid=(b, s // ch, halves),
        in_specs=[blk(0), blk(1), blk(2)],
        out_specs=(pl.BlockSpec((None, ch, ATT_HW), lambda bi, c, hf: (bi, c, hf)),
                   pl.BlockSpec((None, ch, V7X_LANES), lambda bi, c, hf: (bi, c, 0))),
        scratch_shapes=[
            pltpu.VMEM((dil, ATT_BLK, ATT_HW), BF16),
            split_kv,
            split_kv,
            pltpu.VMEM((dil, ATT_BLK, ATT_HW), BF16),
            pltpu.VMEM((ch, V7X_LANES), F32),
        ] + _score_scratch(ATT_RES * ATT_NH),
        compiler_params=_params(3, 56),
        name=f"attn_g{g}",
    )(proj3, proj3, proj3)
    return o.reshape(b * s, GROUP_W), lse.reshape(b * s, V7X_LANES)


CMB_TM = 512


def _combine_kernel(o0_ref, o1_ref, o2_ref, l0_ref, l1_ref, l2_ref, att_ref):
    l0, l1, l2 = l0_ref[...], l1_ref[...], l2_ref[...]
    m = jnp.maximum(jnp.maximum(l0, l1), l2)
    e0, e1, e2 = jnp.exp(l0 - m), jnp.exp(l1 - m), jnp.exp(l2 - m)
    den = e0 + e1 + e2
    for g, (o_ref, e) in enumerate(((o0_ref, e0), (o1_ref, e1), (o2_ref, e2))):
        wgt = e / den
        for h in range(HEADS_PER_GROUP):
            sl = slice(h * HEAD_DIM, (h + 1) * HEAD_DIM)
            val = o_ref[:, sl].astype(F32) * wgt[:, h:h + 1]
            att_ref[:, g * GROUP_W + h * HEAD_DIM:g * GROUP_W + (h + 1) * HEAD_DIM] = val.astype(BF16)


def _combine(os, lses):
    t = os[0].shape[0]
    o_spec = pl.BlockSpec((CMB_TM, GROUP_W), lambda i: (i, 0))
    l_spec = pl.BlockSpec((CMB_TM, V7X_LANES), lambda i: (i, 0))
    return pl.pallas_call(
        _combine_kernel,
        out_shape=jax.ShapeDtypeStruct((t, ATT_W), BF16),
        grid=(t // CMB_TM,),
        in_specs=[o_spec] * 3 + [l_spec] * 3,
        out_specs=pl.BlockSpec((CMB_TM, ATT_W), lambda i: (i, 0)),
        compiler_params=_params(1, 32),
        name="attn_combine",
    )(*os, *lses)


LRU_TT, LRU_CW, LRU_ROWS = 512, 512, 32
LRU_CHUNK = LRU_TT // V7X_SUBLANES
LRU_PAD = (LRU_CONV - 1) * V7X_SUBLANES


def _compose_over_sublanes(a, b, row):
    for s in (1, 2, 4):
        m = row >= s
        b = jnp.where(m, a * pltpu.roll(b, s, axis=0) + b, b)
        a = jnp.where(m, a * pltpu.roll(a, s, axis=0), a)
    return a, b


def _chunk_order_perms(tile):
    row = lax.broadcasted_iota(jnp.int32, (tile, tile), 0)
    col = lax.broadcasted_iota(jnp.int32, (tile, tile), 1)
    chunk = tile // V7X_SUBLANES
    to_chunks = col == (row % V7X_SUBLANES) * chunk + row // V7X_SUBLANES
    to_tokens = col == (row % chunk) * V7X_SUBLANES + row // chunk
    return to_chunks.astype(BF16), to_tokens.astype(BF16)


def _lru_kernel(xr_ref, gy_ref, cw_ref, cb_ref, wr_ref, br_ref, wi_ref, bi_ref, lam_ref, to_chunks_ref,
                to_tokens_ref, o_ref, ext_ref, halo_ref, carry_ref, xc_ref, xcb_ref, a_ref, b_ref):
    tt, sub, pad = LRU_TT, V7X_SUBLANES, LRU_PAD
    row = lax.broadcasted_iota(jnp.int32, (sub, LRU_CW), 0)

    @pl.when(pl.program_id(2) == 0)
    def _():
        halo_ref[...] = jnp.zeros(halo_ref.shape, F32)
        carry_ref[...] = jnp.zeros(carry_ref.shape, F32)

    ext_ref[pad:pad + tt, :] = _dot(to_chunks_ref[...], xr_ref[...])
    for m in range(LRU_CONV - 1):
        cur = ext_ref[tt + m * sub:tt + (m + 1) * sub, :]
        old = halo_ref[m * sub:(m + 1) * sub, :]
        ext_ref[m * sub:(m + 1) * sub, :] = jnp.where(
            row == 0, pltpu.roll(old, 1, axis=0), pltpu.roll(cur, 1, axis=0))
    halo_ref[...] = ext_ref[tt:tt + pad, :]

    cw = cw_ref[...]
    cb = cb_ref[...]
    for r0 in range(0, tt, LRU_ROWS):
        xc = ext_ref[r0:r0 + LRU_ROWS, :] * cw[0:1]
        for m in range(1, LRU_CONV):
            xc = xc + ext_ref[r0 + m * sub:r0 + m * sub + LRU_ROWS, :] * cw[m:m + 1]
        xc = xc + cb
        xc_ref[r0:r0 + LRU_ROWS, :] = xc
        xcb_ref[r0:r0 + LRU_ROWS, :] = xc.astype(BF16)

    for n in range(LRU_CW // LRU_BW):
        cols = slice(n * LRU_BW, (n + 1) * LRU_BW)
        a_ref[:, cols] = _dot(xcb_ref[:, cols], wr_ref[n])
        b_ref[:, cols] = _dot(xcb_ref[:, cols], wi_ref[n])

    lam = lam_ref[...]
    neg_c_softplus = -LRU_C * (jnp.maximum(-lam, 0.0) + jnp.log1p(jnp.exp(-jnp.abs(lam))))
    br, bi = br_ref[...], bi_ref[...]
    for r0 in range(0, tt, LRU_ROWS):
        rows = slice(r0, r0 + LRU_ROWS)
        r = _sigmoid(a_ref[rows, :] + br)
        i = _sigmoid(b_ref[rows, :] + bi)
        log_a = r * neg_c_softplus
        a = jnp.exp(log_a)
        a_ref[rows, :] = a
        b_ref[rows, :] = jnp.sqrt(-jnp.tanh(log_a) * (a * a + 1.0)) * i * xc_ref[rows, :]

    h_end = b_ref[0:sub, :]
    decay = a_ref[0:sub, :]
    for k in range(1, LRU_CHUNK):
        av = a_ref[k * sub:(k + 1) * sub, :]
        h_end = av * h_end + b_ref[k * sub:(k + 1) * sub, :]
        decay = av * decay
    pa, pb = _compose_over_sublanes(decay, h_end, row)
    h_in = carry_ref[...]
    h_out = pa * h_in + pb
    carry_ref[...] = jnp.broadcast_to(h_out[sub - 1:sub, :], (sub, LRU_CW))
    h = jnp.where(row == 0, h_in, pltpu.roll(h_out, 1, axis=0))
    for k in range(LRU_CHUNK):
        h = a_ref[k * sub:(k + 1) * sub, :] * h + b_ref[k * sub:(k + 1) * sub, :]
        xc_ref[k * sub:(k + 1) * sub, :] = h
    gated = (xc_ref[...] * _dot(to_chunks_ref[...], gy_ref[...])).astype(BF16)
    o_ref[...] = _dot(to_tokens_ref[...], gated).astype(BF16)


def _rglru(proj3, cw, cb, wr_bf, br, wi_bf, bi, lam):
    b, s, _ = proj3.shape
    assert OFF_XR % LRU_CW == 0 and OFF_YR % LRU_CW == 0 and s % LRU_TT == 0
    xr_blk, gy_blk = OFF_XR // LRU_CW, OFF_YR // LRU_CW
    gate_blocks = LRU_CW // LRU_BW
    vec = lambda: pl.BlockSpec((1, LRU_CW), lambda n, bi_, t: (0, n))
    gate_w = lambda: pl.BlockSpec((gate_blocks, LRU_BW, LRU_BW), lambda n, bi_, t: (n, 0, 0))
    perm = lambda: pl.BlockSpec((LRU_TT, LRU_TT), lambda n, bi_, t: (0, 0))
    to_chunks, to_tokens = _chunk_order_perms(LRU_TT)
    out = pl.pallas_call(
        _lru_kernel,
        out_shape=jax.ShapeDtypeStruct((b, s, LRU_W), BF16),
        grid=(LRU_W // LRU_CW, b, s // LRU_TT),
        in_specs=[
            pl.BlockSpec((None, LRU_TT, LRU_CW), lambda n, bi_, t: (bi_, t, xr_blk + n)),
            pl.BlockSpec((None, LRU_TT, LRU_CW), lambda n, bi_, t: (bi_, t, gy_blk + n)),
            pl.BlockSpec((LRU_CONV, LRU_CW), lambda n, bi_, t: (0, n)),
            vec(),
            gate_w(),
            vec(),
            gate_w(),
            vec(),
            vec(),
            perm(),
            perm(),
        ],
        out_specs=pl.BlockSpec((None, LRU_TT, LRU_CW), lambda n, bi_, t: (bi_, t, n)),
        scratch_shapes=[
            pltpu.VMEM((LRU_TT + LRU_PAD, LRU_CW), F32),
            pltpu.VMEM((LRU_PAD, LRU_CW), F32),
            pltpu.VMEM((V7X_SUBLANES, LRU_CW), F32),
            pltpu.VMEM((LRU_TT, LRU_CW), F32),
            pltpu.VMEM((LRU_TT, LRU_CW), BF16),
            pltpu.VMEM((LRU_TT, LRU_CW), F32),
            pltpu.VMEM((LRU_TT, LRU_CW), F32),
        ],
        compiler_params=_params(3, 32),
        name="rglru",
    )(proj3, proj3, cw, cb, wr_bf, br, wi_bf, bi, lam, to_chunks, to_tokens)
    return out.reshape(b * s, LRU_W)


MRG_TM, MRG_TN = 1024, 512


def _merge_kernel(att_ref, lru_ref, wa_ref, wl_ref, sa_ref, sr_ref, o_ref):
    ya = _dot(att_ref[...], wa_ref[...])
    yl = _dot(lru_ref[...], wl_ref[...])
    o_ref[...] = (sa_ref[...].astype(F32) * ya + sr_ref[...].astype(F32) * yl).astype(BF16)


def _merge(att, lru, wa_bf, wl_bf, proj):
    t = att.shape[0]
    ga_blk, gr_blk = OFF_GA // MRG_TN, OFF_GR // MRG_TN
    return pl.pallas_call(
        _merge_kernel,
        out_shape=jax.ShapeDtypeStruct((t, D_MODEL), BF16),
        grid=(t // MRG_TM, D_MODEL // MRG_TN),
        in_specs=[
            pl.BlockSpec((MRG_TM, ATT_W), lambda i, j: (i, 0)),
            pl.BlockSpec((MRG_TM, LRU_W), lambda i, j: (i, 0)),
            pl.BlockSpec((ATT_W, MRG_TN), lambda i, j: (0, j)),
            pl.BlockSpec((LRU_W, MRG_TN), lambda i, j: (0, j)),
            pl.BlockSpec((MRG_TM, MRG_TN), lambda i, j: (i, ga_blk + j)),
            pl.BlockSpec((MRG_TM, MRG_TN), lambda i, j: (i, gr_blk + j)),
        ],
        out_specs=pl.BlockSpec((MRG_TM, MRG_TN), lambda i, j: (i, j)),
        compiler_params=_params(2, 52),
        name="merge",
    )(att, lru, wa_bf, wl_bf, proj, proj)


RES_TM, RES_TN = 1024, 512


def _res_matmul_kernel(a_ref, w_ref, x_ref, o_ref):
    o_ref[...] = x_ref[...] + _dot(a_ref[...], w_ref[...])


def _res_matmul(a, w_bf, x2, name):
    t, k = a.shape
    n = w_bf.shape[1]
    return pl.pallas_call(
        _res_matmul_kernel,
        out_shape=jax.ShapeDtypeStruct((t, n), F32),
        grid=(t // RES_TM, n // RES_TN),
        in_specs=[
            pl.BlockSpec((RES_TM, k), lambda i, j: (i, 0)),
            pl.BlockSpec((k, RES_TN), lambda i, j: (0, j)),
            pl.BlockSpec((RES_TM, RES_TN), lambda i, j: (i, j)),
        ],
        out_specs=pl.BlockSpec((RES_TM, RES_TN), lambda i, j: (i, j)),
        compiler_params=_params(2, 52),
        name=name,
    )(a, w_bf, x2)


UP_TM, UP_TN, UP_SUB, UP_ROWS = 512, 1024, 512, 16
UP_CHUNK = UP_TM // V7X_SUBLANES


def _ffn_norm_kernel(x_ref, g_ref, h_ref, stage_ref):
    sub = V7X_SUBLANES
    g = g_ref[...]
    for m in range(UP_TM // sub):
        y = _rmsnorm(x_ref[m * sub:(m + 1) * sub, :], g)
        chunk, k0 = divmod(m * sub, UP_CHUNK)
        for cb in range(D_MODEL // V7X_LANES):
            stage_ref[cb, pl.ds(sub * k0 + chunk, sub, stride=sub), :] = (
                y[:, cb * V7X_LANES:(cb + 1) * V7X_LANES])
    for cb in range(D_MODEL // V7X_LANES):
        h_ref[:, cb * V7X_LANES:(cb + 1) * V7X_LANES] = stage_ref[cb].astype(BF16)


def _ffn_norm(x2, g):
    t = x2.shape[0]
    return pl.pallas_call(
        _ffn_norm_kernel,
        out_shape=jax.ShapeDtypeStruct((t, D_MODEL), BF16),
        grid=(t // UP_TM,),
        in_specs=[pl.BlockSpec((UP_TM, D_MODEL), lambda i: (i, 0)),
                  pl.BlockSpec((1, D_MODEL), lambda i: (0, 0))],
        out_specs=pl.BlockSpec((UP_TM, D_MODEL), lambda i: (i, 0)),
        scratch_shapes=[pltpu.VMEM((D_MODEL // V7X_LANES, UP_TM, V7X_LANES), F32)],
        compiler_params=_params(1, 40),
        name="ffn_norm",
    )(x2, g)


def _ffn_up_kernel(seq_tiles, h_ref, wg_ref, wv_ref, cwg_ref, cwv_ref, cbg_ref, cbv_ref,
                   o_ref, ext_g_ref, ext_v_ref, halo_g_ref, halo_v_ref, gact_ref):
    i, j = pl.program_id(0), pl.program_id(1)
    tm = UP_TM
    sub = V7X_SUBLANES
    pad = (FFN_CONV - 1) * sub

    @pl.when((i == 0) & (j == 0))
    def _():
        halo_g_ref[...] = jnp.zeros(halo_g_ref.shape, F32)
        halo_v_ref[...] = jnp.zeros(halo_v_ref.shape, F32)

    seq_start = i % seq_tiles == 0
    h = h_ref[...]

    def raw(w_ref, ext_ref, halo_ref, sl):
        ext_ref[pad:pad + tm, sl] = _dot(h, w_ref[:, sl])
        first_row = lax.broadcasted_iota(jnp.int32, (sub, sl.stop - sl.start), 0) == 0
        for m in range(FFN_CONV - 1):
            cur = ext_ref[tm + m * sub:tm + (m + 1) * sub, sl]
            old = jnp.where(seq_start, 0.0, halo_ref[j, m * sub:(m + 1) * sub, sl])
            ext_ref[m * sub:(m + 1) * sub, sl] = jnp.where(
                first_row, pltpu.roll(old, 1, axis=0), pltpu.roll(cur, 1, axis=0))
        halo_ref[j, :, sl] = ext_ref[tm:tm + pad, sl]

    def conv(ext_ref, cw, cb, r0, sl):
        y = ext_ref[r0:r0 + UP_ROWS, sl] * cw[0:1]
        for m in range(1, FFN_CONV):
            y = y + ext_ref[r0 + m * sub:r0 + m * sub + UP_ROWS, sl] * cw[m:m + 1]
        return y + cb

    full = slice(0, UP_TN)
    raw(wg_ref, ext_g_ref, halo_g_ref, full)
    cwg, cbg = cwg_ref[...], cbg_ref[...]
    for r0 in range(0, tm, UP_ROWS):
        gact_ref[r0:r0 + UP_ROWS, :] = _gelu_tanh(conv(ext_g_ref, cwg, cbg, r0, full))
    for s in range(0, UP_TN, UP_SUB):
        sl = slice(s, s + UP_SUB)
        raw(wv_ref, ext_v_ref, halo_v_ref, sl)
        cwv, cbv = cwv_ref[:, sl], cbv_ref[:, sl]
        for r0 in range(0, tm, UP_ROWS):
            val = conv(ext_v_ref, cwv, cbv, r0, sl)
            o_ref[r0:r0 + UP_ROWS, sl] = (gact_ref[r0:r0 + UP_ROWS, sl] * val).astype(BF16)


def _ffn_up(h, w_bf, cw, cb, seq):
    t = h.shape[0]
    nj = FFN_DIM // UP_TN
    pad = (FFN_CONV - 1) * V7X_SUBLANES
    assert seq % UP_TM == 0
    return pl.pallas_call(
        functools.partial(_ffn_up_kernel, seq // UP_TM),
        out_shape=jax.ShapeDtypeStruct((t, FFN_DIM), BF16),
        grid=(t // UP_TM, nj),
        in_specs=[
            pl.BlockSpec((UP_TM, D_MODEL), lambda i, j: (i, 0)),
            pl.BlockSpec((D_MODEL, UP_TN), lambda i, j: (0, j)),
            pl.BlockSpec((D_MODEL, UP_TN), lambda i, j: (0, nj + j)),
            pl.BlockSpec((FFN_CONV, UP_TN), lambda i, j: (0, j)),
            pl.BlockSpec((FFN_CONV, UP_TN), lambda i, j: (0, nj + j)),
            pl.BlockSpec((1, UP_TN), lambda i, j: (0, j)),
            pl.BlockSpec((1, UP_TN), lambda i, j: (0, nj + j)),
        ],
        out_specs=pl.BlockSpec((UP_TM, UP_TN), lambda i, j: (i, j)),
        scratch_shapes=[
            pltpu.VMEM((UP_TM + pad, UP_TN), F32),
            pltpu.VMEM((UP_TM + pad, UP_TN), F32),
            pltpu.VMEM((nj, pad, UP_TN), F32),
            pltpu.VMEM((nj, pad, UP_TN), F32),
            pltpu.VMEM((UP_TM, UP_TN), F32),
        ],
        compiler_params=_params(2, 56),
        name="ffn_up",
    )(h, w_bf, w_bf, cw, cw, cb, cb)


DN_TN, DN_SUB = 512, 256


def _ffn_down_kernel(a_ref, w_ref, x_ref, o_ref, acc_ref):
    sub = V7X_SUBLANES
    a = a_ref[...]
    for s in range(0, DN_TN, DN_SUB):
        d = _dot(a, w_ref[:, s:s + DN_SUB])
        lane_blocks = [(c // V7X_LANES, slice(c, c + V7X_LANES)) for c in range(s, s + DN_SUB, V7X_LANES)]
        for c, cols in lane_blocks:
            acc_ref[c] = d[:, cols.start - s:cols.stop - s]
        for m in range(UP_TM // sub):
            chunk, k0 = divmod(m * sub, UP_CHUNK)
            rows = slice(m * sub, (m + 1) * sub)
            src = pl.ds(sub * k0 + chunk, sub, stride=sub)
            for c, cols in lane_blocks:
                o_ref[rows, cols] = x_ref[rows, cols] + acc_ref[c, src, :]


def _ffn_down(a, w_bf, x2):
    t, kk = a.shape
    n = w_bf.shape[1]
    return pl.pallas_call(
        _ffn_down_kernel,
        out_shape=jax.ShapeDtypeStruct((t, n), F32),
        grid=(t // UP_TM, n // DN_TN),
        in_specs=[
            pl.BlockSpec((UP_TM, kk), lambda i, j: (i, 0)),
            pl.BlockSpec((kk, DN_TN), lambda i, j: (0, j)),
            pl.BlockSpec((UP_TM, DN_TN), lambda i, j: (i, j)),
        ],
        out_specs=pl.BlockSpec((UP_TM, DN_TN), lambda i, j: (i, j)),
        scratch_shapes=[pltpu.VMEM((DN_TN // V7X_LANES, UP_TM, V7X_LANES), F32)],
        compiler_params=_params(2, 56),
        name="ffn_down",
    )(a, w_bf, x2)


PLE_TM, PLE_TN = 512, 512


def _ple_kernel(x_ref, p_ref, gg_ref, wg_ref, wp_ref, ge_ref, gf_ref, o_ref, h_ref):
    j = pl.program_id(1)

    @pl.when(j == 0)
    def _():
        _rmsnorm_rows(x_ref, gg_ref[...], h_ref)
        for c in range(0, D_MODEL, PLE_TN):
            o_ref[:, c:c + PLE_TN] = _dot(p_ref[...], wp_ref[:, c:c + PLE_TN])
        _rmsnorm_rows(o_ref, ge_ref[...], o_ref)

    col = pl.ds(pl.multiple_of(j * PLE_TN, PLE_TN), PLE_TN)
    gate = _sigmoid(_dot(h_ref[...], wg_ref[...]))
    o_ref[:, col] = x_ref[:, col] + gate * o_ref[:, col]

    @pl.when(j == pl.num_programs(1) - 1)
    def _():
        _rmsnorm_rows(o_ref, gf_ref[...], o_ref)


def _ple(x2, p_bf, gg, wg_bf, wp_bf, ge, gf):
    t = x2.shape[0]
    row = lambda: pl.BlockSpec((1, D_MODEL), lambda i, j: (0, 0))
    return pl.pallas_call(
        _ple_kernel,
        out_shape=jax.ShapeDtypeStruct((t, D_MODEL), F32),
        grid=(t // PLE_TM, D_MODEL // PLE_TN),
        in_specs=[
            pl.BlockSpec((PLE_TM, D_MODEL), lambda i, j: (i, 0)),
            pl.BlockSpec((PLE_TM, PLE_DIM), lambda i, j: (i, 0)),
            row(),
            pl.BlockSpec((D_MODEL, PLE_TN), lambda i, j: (0, j)),
            pl.BlockSpec((PLE_DIM, D_MODEL), lambda i, j: (0, 0)),
            row(),
            row(),
        ],
        out_specs=pl.BlockSpec((PLE_TM, D_MODEL), lambda i, j: (i, 0)),
        scratch_shapes=[pltpu.VMEM((PLE_TM, D_MODEL), BF16)],
        compiler_params=_params(2, 56),
        name="ple_final",
    )(x2, p_bf, gg, wg_bf, wp_bf, ge, gf)


def kernel(x, p, mix_norm_g, w_in, lru_conv_w, lru_conv_b, w_rgate, b_rgate, w_igate, b_igate, lru_lambda, w_att_branch, w_lru_branch, w_out, ffn_norm_g, w_up, ffn_conv_w, ffn_conv_b, w_down, ple_proj, ple_norm_g, ple_gate_norm_g, ple_gate_w, final_norm_g):
    b, s, d = x.shape
    depth = w_in.shape[0]
    t = b * s
    row = lambda v: v.reshape(1, -1)
    xf = x.reshape(t, d)
    for l in range(depth):
        proj = _in_proj(xf, row(mix_norm_g[l]), w_in[l].astype(BF16))
        proj3 = proj.reshape(b, s, IN_COLS)
        os, lses = zip(*[_attention_dense(proj3, g) if dil == 1 else _attention_strided(proj3, g, dil)
                         for g, (_, dil) in enumerate(PATTERNS)])
        att = _combine(os, lses)
        lru = _rglru(proj3, lru_conv_w[l], row(lru_conv_b[l]), w_rgate[l].astype(BF16), row(b_rgate[l]),
                     w_igate[l].astype(BF16), row(b_igate[l]), row(lru_lambda[l]))
        merged = _merge(att, lru, w_att_branch[l].astype(BF16), w_lru_branch[l].astype(BF16), proj)
        xf = _res_matmul(merged, w_out[l].astype(BF16), xf, "out_proj")
        hn = _ffn_norm(xf, row(ffn_norm_g[l]))
        gact = _ffn_up(hn, w_up[l].astype(BF16), ffn_conv_w[l], row(ffn_conv_b[l]), s)
        xf = _ffn_down(gact, w_down[l].astype(BF16), xf)
        assert l == depth - 1 == 0
        xf = _ple(xf, p[l].reshape(t, PLE_DIM).astype(BF16), row(ple_gate_norm_g[l]),
                  ple_gate_w[l].astype(BF16), ple_proj[l].astype(BF16), row(ple_norm_g[l]), row(final_norm_g))
    return xf.reshape(b, s, d)
```
